```python
import jax, jax.numpy as jnp
from jax import lax
import numpy as np

D_MODEL = 1024
BATCH = 32
SEQ = 2048
DEPTH = 4
DEC_BATCH = 8
DEC_SEQ = 16
PAST_LEN = 4096

CHUNK = 64
N_MIXERS = 2
N_CONV_LAYERS = (DEPTH + 1) // 2
N_HGRN_LAYERS = DEPTH // 2
CONV_K = 31
HG_HEADS = 8
HG_DK = D_MODEL // HG_HEADS
HG_DV = D_MODEL // HG_HEADS
HG_BLOCK = 16
D_FF = 2816
FFN_K = 3
EPS = 1e-6

kernel_name = 'hybrid_conformer_hgrn2_stream_step'


def rmsnorm(x, g):
    xf = x.astype(jnp.float32)
    y = xf * lax.rsqrt(jnp.mean(xf * xf, axis=-1, keepdims=True) + EPS)
    return (y * g.astype(jnp.float32)).astype(x.dtype)


def layernorm(x, g, b):
    xf = x.astype(jnp.float32)
    mu = jnp.mean(xf, axis=-1, keepdims=True)
    var = jnp.mean(jnp.square(xf - mu), axis=-1, keepdims=True)
    y = (xf - mu) * lax.rsqrt(var + EPS)
    return (y * g.astype(jnp.float32) + b.astype(jnp.float32)).astype(x.dtype)


def causal_dwconv(xp, w, b):
    c = xp.shape[-1]
    y = lax.conv_general_dilated(xp, w[:, None, :].astype(xp.dtype), (1,), 'VALID',
                                 dimension_numbers=('NWC', 'WIO', 'NWC'),
                                 feature_group_count=c)
    return y + b


def conv_module(h, buf, w_pw1, b_pw1, w_dw, b_dw, ln_g, ln_b, w_pw2, b_pw2):
    u = h @ w_pw1 + b_pw1
    a, gt = jnp.split(u, 2, axis=-1)
    glu = a * jax.nn.sigmoid(gt)
    xp = jnp.concatenate([buf.astype(glu.dtype), glu], axis=1)
    new_buf = xp[:, xp.shape[1] - (CONV_K - 1):]
    y = causal_dwconv(xp, w_dw, b_dw)
    y = jax.nn.silu(layernorm(y, ln_g, ln_b))
    return y @ w_pw2 + b_pw2, new_buf


def _to_blocks(a, pad, c):
    bsz, t, _ = a.shape
    a = jnp.pad(a, ((0, 0), (0, pad), (0, 0)))
    a = a.reshape(bsz, (t + pad) // HG_BLOCK, HG_BLOCK, HG_HEADS, c)
    return a.transpose(1, 0, 3, 2, 4)


def hgrn2_mix(h, s0, w_in, lb, norm_w, w_o):
    bsz, t, _ = h.shape
    proj = (h @ w_in).astype(jnp.float32)
    q, fl, iv, g = jnp.split(proj, 4, axis=-1)
    f = lb + (1.0 - lb) * jax.nn.sigmoid(fl)
    log_f = jnp.log(f)
    k = 1.0 - f
    pad = (-t) % HG_BLOCK
    qb = _to_blocks(q, pad, HG_DK)
    kb = _to_blocks(k, pad, HG_DK)
    ib = _to_blocks(iv, pad, HG_DV)
    lfb = _to_blocks(log_f, pad, HG_DK)
    mask = jnp.tril(jnp.ones((HG_BLOCK, HG_BLOCK), dtype=bool))

    def step(s, blk):
        qc, kc, ic, lfc = blk
        bcum = jnp.cumsum(lfc, axis=2)
        diff = bcum[:, :, :, None, :] - bcum[:, :, None, :, :]
        decay = jnp.where(mask[:, :, None], jnp.exp(jnp.minimum(diff, 0.0)), 0.0)
        scores = jnp.einsum('bhtk,bhsk,bhtsk->bhts', qc, kc, decay)
        o = (jnp.einsum('bhts,bhsv->bhtv', scores, ic)
             + jnp.einsum('bhtk,bhkv->bhtv', qc * jnp.exp(bcum), s))
        b_last = bcum[:, :, -1]
        s = (jnp.exp(b_last)[..., None] * s
             + jnp.einsum('bhsk,bhsv->bhkv', kc * jnp.exp(b_last[:, :, None] - bcum), ic))
        return s, o

    s_fin, o = lax.scan(step, s0.astype(jnp.float32), (qb, kb, ib, lfb))
    o = o.transpose(1, 0, 3, 2, 4).reshape(bsz, -1, HG_HEADS, HG_DV)[:, :t]
    o = o * lax.rsqrt(jnp.mean(o * o, axis=-1, keepdims=True) + EPS)
    o = o.reshape(bsz, t, D_MODEL) * norm_w.astype(jnp.float32) * jax.nn.silu(g)
    return o.astype(h.dtype) @ w_o, s_fin.astype(s0.dtype)


def conv_ffn(h, buf, w_up, b_up, w_dw, b_dw, w_down, b_down):
    u = h @ w_up + b_up
    xp = jnp.concatenate([buf.astype(u.dtype), u], axis=1)
    new_buf = xp[:, xp.shape[1] - (FFN_K - 1):]
    c = causal_dwconv(xp, w_dw, b_dw)
    a, v = jnp.split(c, 2, axis=-1)
    return (jax.nn.silu(a) * v) @ w_down + b_down, new_buf


def trunk(x, conv_bufs, hgrn_states, ffn_bufs, P):
    lb_p = jax.nn.softmax(P['hg_lb_logits'].astype(jnp.float32), axis=0)
    lbs = jnp.cumsum(lb_p, axis=0) - lb_p[0]
    conv_new, hgrn_new, ffn_new = [], [], []
    for i in range(DEPTH):
        j = i // N_MIXERS
        h = rmsnorm(x, P['norm_mix_w'][i])
        if i % N_MIXERS == 0:
            out, nb = conv_module(h, conv_bufs[j], P['cv_w_pw1'][j], P['cv_b_pw1'][j],
                                  P['cv_w_dw'][j], P['cv_b_dw'][j], P['cv_ln_g'][j],
                                  P['cv_ln_b'][j], P['cv_w_pw2'][j], P['cv_b_pw2'][j])
            conv_new.append(nb)
        else:
            out, ns = hgrn2_mix(h, hgrn_states[j], P['hg_w_in'][j], lbs[j],
                                P['hg_norm_w'][j], P['hg_w_o'][j])
            hgrn_new.append(ns)
        x = x + out
        h = rmsnorm(x, P['norm_ffn_w'][i])
        out, fb = conv_ffn(h, ffn_bufs[i], P['ff_w_up'][i], P['ff_b_up'][i], P['ff_w_dw'][i],
                           P['ff_b_dw'][i], P['ff_w_down'][i], P['ff_b_down'][i])
        ffn_new.append(fb)
        x = x + out
    y = rmsnorm(x, P['norm_final_w'])
    return y, jnp.stack(conv_new), jnp.stack(hgrn_new), jnp.stack(ffn_new)


def setup_inputs(seed: int = 0) -> dict:
    key = jax.random.key(seed)
    ks = iter(jax.random.split(key, 32))

    def nrm(shape, scale):
        return jax.random.normal(next(ks), shape, jnp.float32) * scale

    D, F = D_MODEL, D_FF
    NC, NH = N_CONV_LAYERS, N_HGRN_LAYERS
    return {
        'x_prompt': nrm((BATCH, SEQ, D), 1.0),
        'x_sample': nrm((DEC_BATCH, DEC_SEQ, D), 1.0),
        'cache_conv': nrm((NC, DEC_BATCH, CONV_K - 1, D), 0.5),
        'state_hgrn': nrm((NH, DEC_BATCH, HG_HEADS, HG_DK, HG_DV), 0.5),
        'cache_ffn': nrm((DEPTH, DEC_BATCH, FFN_K - 1, 2 * F), 1.0),
        'norm_mix_w': 1.0 + nrm((DEPTH, D), 0.02),
        'norm_ffn_w': 1.0 + nrm((DEPTH, D), 0.02),
        'norm_final_w': 1.0 + nrm((D,), 0.02),
        'cv_w_pw1': nrm((NC, D, 2 * D), D ** -0.5),
        'cv_b_pw1': nrm((NC, 2 * D), 0.02),
        'cv_w_dw': nrm((NC, CONV_K, D), CONV_K ** -0.5),
        'cv_b_dw': nrm((NC, D), 0.02),
        'cv_ln_g': 1.0 + nrm((NC, D), 0.02),
        'cv_ln_b': nrm((NC, D), 0.02),
        'cv_w_pw2': nrm((NC, D, D), D ** -0.5),
        'cv_b_pw2': nrm((NC, D), 0.02),
        'hg_w_in': nrm((NH, D, 4 * D), D ** -0.5),
        'hg_lb_logits': nrm((NH, D), 0.5),
        'hg_norm_w': 1.0 + nrm((NH, D), 0.02),
        'hg_w_o': nrm((NH, D, D), D ** -0.5),
        'ff_w_up': nrm((DEPTH, D, 2 * F), D ** -0.5),
        'ff_b_up': nrm((DEPTH, 2 * F), 0.02),
        'ff_w_dw': nrm((DEPTH, FFN_K, 2 * F), FFN_K ** -0.5),
        'ff_b_dw': nrm((DEPTH, 2 * F), 0.02),
        'ff_w_down': nrm((DEPTH, F, D), F ** -0.5),
        'ff_b_down': nrm((DEPTH, D), 0.02),
    }


def reference(x_prompt, x_sample, cache_conv, state_hgrn, cache_ffn,
              norm_mix_w, norm_ffn_w, norm_final_w,
              cv_w_pw1, cv_b_pw1, cv_w_dw, cv_b_dw, cv_ln_g, cv_ln_b, cv_w_pw2, cv_b_pw2,
              hg_w_in, hg_lb_logits, hg_norm_w, hg_w_o,
              ff_w_up, ff_b_up, ff_w_dw, ff_b_dw, ff_w_down, ff_b_down):
    P = dict(norm_mix_w=norm_mix_w, norm_ffn_w=norm_ffn_w, norm_final_w=norm_final_w,
             cv_w_pw1=cv_w_pw1, cv_b_pw1=cv_b_pw1, cv_w_dw=cv_w_dw, cv_b_dw=cv_b_dw,
             cv_ln_g=cv_ln_g, cv_ln_b=cv_ln_b, cv_w_pw2=cv_w_pw2, cv_b_pw2=cv_b_pw2,
             hg_w_in=hg_w_in, hg_lb_logits=hg_lb_logits, hg_norm_w=hg_norm_w, hg_w_o=hg_w_o,
             ff_w_up=ff_w_up, ff_b_up=ff_b_up, ff_w_dw=ff_w_dw, ff_b_dw=ff_b_dw,
             ff_w_down=ff_w_down, ff_b_down=ff_b_down)
    bp = x_prompt.shape[0]
    zero_conv = jnp.zeros((N_CONV_LAYERS, bp, CONV_K - 1, D_MODEL), x_prompt.dtype)
    zero_hgrn = jnp.zeros((N_HGRN_LAYERS, bp, HG_HEADS, HG_DK, HG_DV), jnp.float32)
    zero_ffn = jnp.zeros((DEPTH, bp, FFN_K - 1, 2 * D_FF), x_prompt.dtype)
    y_prompt, conv_p, hgrn_p, ffn_p = trunk(x_prompt, zero_conv, zero_hgrn, zero_ffn, P)
    y_sample, conv_s, hgrn_s, ffn_s = trunk(x_sample, cache_conv, state_hgrn, cache_ffn, P)
    return (y_prompt, y_sample, conv_p, hgrn_p, ffn_p, conv_s, hgrn_s, ffn_s)
```

```python
import functools

import jax
import jax.numpy as jnp
from jax import lax
from jax.experimental import pallas as pl
from jax.experimental.pallas import tpu as pltpu

D_MODEL = 1024
CONV_K = 31
HG_HEADS = 8
HG_DH = D_MODEL // HG_HEADS
D_FF = 2816
FFN_K = 3
EPS = 1e-6

LANES = 128
SUBLANES = 8
MXU_N = 256
VMEM_LIMIT_BYTES = 56 * 1024 * 1024

CONV_HIST = CONV_K - 1
CONV_PAD = 32
FFN_HIST = FFN_K - 1
FFN_PAD = SUBLANES
PROMPT_TILE = 512
HG_CHUNK = 32
MIN_CHUNK_LOG_DECAY = -80.0
ROW_GROUP = 64
NORM_ROWS = 16

F32 = jnp.float32
BF16 = jnp.bfloat16


def _dot(a, b):
    return jnp.dot(a, b, preferred_element_type=F32)


def _dot_nt(a, b):
    return lax.dot_general(a, b, (((1,), (1,)), ((), ())), preferred_element_type=F32)


def _dot_tn(a, b):
    return lax.dot_general(a, b, (((0,), (0,)), ((), ())), preferred_element_type=F32)


def _rmsnorm_rows(x, w):
    ms = jnp.mean(x * x, axis=-1, keepdims=True)
    return (x * lax.rsqrt(ms + EPS)) * w


def _rmsnorm_to(x_ref, w_ref, h_ref, rows):
    step = min(NORM_ROWS, rows)

    def body(g, carry):
        r = pl.multiple_of(g * step, step)
        h_ref[pl.ds(r, step), :] = _rmsnorm_rows(x_ref[pl.ds(r, step), :], w_ref[...]).astype(BF16)
        return carry

    lax.fori_loop(0, rows // step, body, 0)


def _conv_mixer_kernel(x_ref, hist_ref, nw_ref, w1_ref, b1_ref, wdw_ref, bdw_ref, lng_ref, lnb_ref,
                       w2_ref, b2_ref, xo_ref, histo_ref, h_s, p_s, y_s, z_s, *, rows):
    D = D_MODEL
    nlb = D // LANES
    first = CONV_PAD - CONV_HIST

    @pl.when(pl.program_id(1) == 0)
    def _():
        for lb in range(nlb):
            p_s[lb, pl.ds(first, CONV_HIST), :] = hist_ref[:, lb * LANES:(lb + 1) * LANES]

    _rmsnorm_to(x_ref, nw_ref, h_s, rows)

    h = h_s[...]
    per = MXU_N // LANES
    for cb in range(D // MXU_N):
        ca = slice(cb * MXU_N, (cb + 1) * MXU_N)
        cg = slice(D + cb * MXU_N, D + (cb + 1) * MXU_N)
        ua = _dot(h, w1_ref[:, ca]) + b1_ref[:, ca]
        ug = _dot(h, w1_ref[:, cg]) + b1_ref[:, cg]
        glu = ua * jax.nn.sigmoid(ug)
        for j in range(per):
            p_s[cb * per + j, pl.ds(CONV_PAD, rows), :] = glu[:, j * LANES:(j + 1) * LANES]

    rg = min(ROW_GROUP, rows)

    def conv_body(lb, carry):
        taps = [jnp.broadcast_to(wdw_ref[lb, k:k + 1, :], (rg, LANES)) for k in range(CONV_K)]
        bias = jnp.broadcast_to(bdw_ref[lb], (rg, LANES))
        for g in range(rows // rg):
            acc = bias
            for k in range(CONV_K):
                acc = acc + taps[k] * p_s[lb, pl.ds(g * rg + first + k, rg), :]
            y_s[lb, pl.ds(g * rg, rg), :] = acc
        return carry

    lax.fori_loop(0, nlb, conv_body, 0)

    nr = min(NORM_ROWS, rows)

    def ln_body(g, carry):
        r = pl.multiple_of(g * nr, nr)
        y = jnp.concatenate([y_s[lb, pl.ds(r, nr), :] for lb in range(nlb)], axis=1)
        mu = jnp.mean(y, axis=-1, keepdims=True)
        d = y - mu
        var = jnp.mean(d * d, axis=-1, keepdims=True)
        yn = (d * lax.rsqrt(var + EPS)) * lng_ref[...] + lnb_ref[...]
        z_s[pl.ds(r, nr), :] = (yn * jax.nn.sigmoid(yn)).astype(BF16)
        return carry

    lax.fori_loop(0, rows // nr, ln_body, 0)

    for lb in range(nlb):
        tail = p_s[lb, pl.ds(rows + first, CONV_HIST), :]
        p_s[lb, pl.ds(first, CONV_HIST), :] = tail
        histo_ref[:, lb * LANES:(lb + 1) * LANES] = tail

    out = _dot(z_s[...], w2_ref[...]) + b2_ref[...]
    xo_ref[...] = x_ref[...] + out


def _const_spec(shape):
    zeros = (0,) * len(shape)
    return pl.BlockSpec(shape, lambda b, s: zeros, pipeline_mode=pl.Buffered(1))


def _params():
    return pltpu.CompilerParams(dimension_semantics=("arbitrary", "arbitrary"),
                                vmem_limit_bytes=VMEM_LIMIT_BYTES)


def _row2(v):
    return v.reshape(1, -1)


def _lane_blocks(v):
    r, d = v.shape
    return v.reshape(r, d // LANES, LANES).transpose(1, 0, 2)


def _conv_mixer(x, hist, nw, w1, b1, wdw, bdw, lng, lnb, w2, b2, *, rows):
    B, S, D = x.shape
    grid = (B, S // rows)
    x_spec = pl.BlockSpec((None, rows, D), lambda b, s: (b, s, 0))
    hist_spec = pl.BlockSpec((None, CONV_HIST, D), lambda b, s: (b, 0, 0))
    return pl.pallas_call(
        functools.partial(_conv_mixer_kernel, rows=rows),
        grid=grid,
        in_specs=[x_spec, hist_spec, _const_spec((1, D)), _const_spec((D, 2 * D)),
                  _const_spec((1, 2 * D)), _const_spec((D // LANES, CONV_K, LANES)),
                  _const_spec((D // LANES, 1, LANES)), _const_spec((1, D)), _const_spec((1, D)),
                  _const_spec((D, D)), _const_spec((1, D))],
        out_specs=[x_spec, hist_spec],
        out_shape=[jax.ShapeDtypeStruct(x.shape, x.dtype),
                   jax.ShapeDtypeStruct(hist.shape, hist.dtype)],
        scratch_shapes=[pltpu.VMEM((rows, D), BF16),
                        pltpu.VMEM((D // LANES, CONV_PAD + rows, LANES), F32),
                        pltpu.VMEM((D // LANES, rows, LANES), F32),
                        pltpu.VMEM((rows, D), BF16)],
        compiler_params=_params(),
        name="conv_mixer",
    )(x, hist, _row2(nw), w1.astype(BF16), _row2(b1), _lane_blocks(wdw), _lane_blocks(_row2(bdw)),
      _row2(lng), _row2(lnb), w2.astype(BF16), _row2(b2))


def _hgrn_mixer_kernel(x_ref, s0_ref, nw_ref, win_ref, lbl_ref, hnw_ref, wo_ref, xo_ref, so_ref,
                       h_s, q_s, k_s, i_s, lf_s, bc_s, g_s, o_s, z_s, st_s, *, rows, chunk, layer):
    D = D_MODEL
    nh = HG_HEADS
    dh = HG_DH

    @pl.when(pl.program_id(1) == 0)
    def _():
        for hd in range(nh):
            st_s[hd] = s0_ref[hd].T

    _rmsnorm_to(x_ref, nw_ref, h_s, rows)

    logits = lbl_ref[...]
    m = jnp.max(logits, axis=0, keepdims=True)
    e = jnp.exp(logits - m)
    p = e / jnp.sum(e, axis=0, keepdims=True)
    csum = p[0:1, :]
    for j in range(1, layer + 1):
        csum = csum + p[j:j + 1, :]
    lb = csum - p[0:1, :]

    h = h_s[...]
    for cb in range(D // MXU_N):
        cols = slice(cb * MXU_N, (cb + 1) * MXU_N)
        q_s[:, cols] = _dot(h, win_ref[:, cb * MXU_N:(cb + 1) * MXU_N])
        fl = _dot(h, win_ref[:, D + cb * MXU_N:D + (cb + 1) * MXU_N])
        f = lb[:, cols] + (1.0 - lb[:, cols]) * jax.nn.sigmoid(fl)
        lf_s[:, cols] = jnp.log(f)
        k_s[:, cols] = 1.0 - f
        i_s[:, cols] = _dot(h, win_ref[:, 2 * D + cb * MXU_N:2 * D + (cb + 1) * MXU_N]).astype(BF16)
        g_s[:, cols] = _dot(h, win_ref[:, 3 * D + cb * MXU_N:3 * D + (cb + 1) * MXU_N])

    ri = lax.broadcasted_iota(jnp.int32, (chunk, chunk), 0)
    ci = lax.broadcasted_iota(jnp.int32, (chunk, chunk), 1)
    causal = ri >= ci
    tri = causal.astype(BF16)

    def cum_body(c, mn):
        r = pl.multiple_of(c * chunk, chunk)
        lf = lf_s[pl.ds(r, chunk), :]
        lf_hi = lf.astype(BF16)
        lf_lo = (lf - lf_hi.astype(F32)).astype(BF16)
        bc = _dot(tri, lf_hi) + _dot(tri, lf_lo)
        bc_s[pl.ds(r, chunk), :] = bc
        return jnp.minimum(mn, bc[chunk - 1:chunk, :])

    mn = lax.fori_loop(0, rows // chunk, cum_body, jnp.zeros((1, D), F32))
    safe = jnp.min(mn) >= MIN_CHUNK_LOG_DECAY

    def chunk_body(c, carry):
        r = pl.multiple_of(c * chunk, chunk)
        bc = bc_s[pl.ds(r, chunk), :]
        b_last = bc[chunk - 1:chunk, :]
        qd = (q_s[pl.ds(r, chunk), :] * jnp.exp(bc)).astype(BF16)
        kk = k_s[pl.ds(r, chunk), :]
        kd = (kk * jnp.exp(-bc)).astype(BF16)
        ke = (kk * jnp.exp(b_last - bc)).astype(BF16)
        dec = jnp.exp(b_last)
        iv = i_s[pl.ds(r, chunk), :]
        for hd in range(nh):
            cols = slice(hd * dh, (hd + 1) * dh)
            st = st_s[hd]
            sc = jnp.where(causal, _dot_nt(qd[:, cols], kd[:, cols]), 0.0)
            o = _dot(sc.astype(BF16), iv[:, cols]) + _dot_nt(qd[:, cols], st.astype(BF16))
            o_s[pl.ds(r, chunk), cols] = o
            st_s[hd] = st * dec[:, cols] + _dot_tn(iv[:, cols], ke[:, cols])
        return carry

    row_id = lax.broadcasted_iota(jnp.int32, (chunk, 1), 0)

    def slow_chunk_body(c, carry):
        r = pl.multiple_of(c * chunk, chunk)
        lf = lf_s[pl.ds(r, chunk), :]
        qc = q_s[pl.ds(r, chunk), :].astype(BF16)
        kc = k_s[pl.ds(r, chunk), :].astype(BF16)
        iv = i_s[pl.ds(r, chunk), :]
        o_s[pl.ds(r, chunk), :] = jnp.zeros((chunk, D), F32)

        def token_body(t, carry2):
            sel = row_id == t
            f_row = jnp.exp(jnp.sum(jnp.where(sel, lf, 0.0), axis=0, keepdims=True))
            i_sel = jnp.where(sel, iv, jnp.zeros_like(iv))
            for hd in range(nh):
                cols = slice(hd * dh, (hd + 1) * dh)
                st = st_s[hd] * f_row[:, cols] + _dot_tn(i_sel[:, cols], kc[:, cols])
                st_s[hd] = st
                res = _dot_nt(qc[:, cols], st.astype(BF16))
                o_s[pl.ds(r, chunk), cols] = jnp.where(sel, res, o_s[pl.ds(r, chunk), cols])
            return carry2

        lax.fori_loop(0, chunk, token_body, 0)
        return carry

    @pl.when(safe)
    def _():
        lax.fori_loop(0, rows // chunk, chunk_body, 0)

    @pl.when(jnp.logical_not(safe))
    def _():
        lax.fori_loop(0, rows // chunk, slow_chunk_body, 0)

    nr = min(NORM_ROWS, rows)

    def gate_body(g, carry):
        r = pl.multiple_of(g * nr, nr)
        for hd in range(nh):
            cols = slice(hd * dh, (hd + 1) * dh)
            o = o_s[pl.ds(r, nr), cols]
            ms = jnp.mean(o * o, axis=-1, keepdims=True)
            gt = g_s[pl.ds(r, nr), cols]
            on = (o * lax.rsqrt(ms + EPS)) * hnw_ref[:, cols] * (gt * jax.nn.sigmoid(gt))
            z_s[pl.ds(r, nr), cols] = on.astype(BF16)
        return carry

    lax.fori_loop(0, rows // nr, gate_body, 0)

    xo_ref[...] = x_ref[...] + _dot(z_s[...], wo_ref[...])

    @pl.when(pl.program_id(1) == pl.num_programs(1) - 1)
    def _():
        for hd in range(nh):
            so_ref[hd] = st_s[hd].T


def _hgrn_mixer(x, s0, nw, win, lb_logits, hnw, wo, *, rows, layer):
    B, S, D = x.shape
    chunk = min(HG_CHUNK, rows)
    grid = (B, S // rows)
    x_spec = pl.BlockSpec((None, rows, D), lambda b, s: (b, s, 0))
    st_spec = pl.BlockSpec((None, HG_HEADS, HG_DH, HG_DH), lambda b, s: (b, 0, 0, 0))
    n_layers = lb_logits.shape[0]
    return pl.pallas_call(
        functools.partial(_hgrn_mixer_kernel, rows=rows, chunk=chunk, layer=layer),
        grid=grid,
        in_specs=[x_spec, st_spec, _const_spec((1, D)), _const_spec((D, 4 * D)),
                  _const_spec((n_layers, D)), _const_spec((1, D)), _const_spec((D, D))],
        out_specs=[x_spec, st_spec],
        out_shape=[jax.ShapeDtypeStruct(x.shape, x.dtype),
                   jax.ShapeDtypeStruct(s0.shape, s0.dtype)],
        scratch_shapes=[pltpu.VMEM((rows, D), BF16),
                        pltpu.VMEM((rows, D), F32),
                        pltpu.VMEM((rows, D), F32),
                        pltpu.VMEM((rows, D), BF16),
                        pltpu.VMEM((rows, D), F32),
                        pltpu.VMEM((rows, D), F32),
                        pltpu.VMEM((rows, D), F32),
                        pltpu.VMEM((rows, D), F32),
                        pltpu.VMEM((rows, D), BF16),
                        pltpu.VMEM((HG_HEADS, HG_DH, HG_DH), F32)],
        compiler_params=_params(),
        name="hgrn_mixer",
    )(x, s0, _row2(nw), win.astype(BF16), lb_logits, _row2(hnw), wo.astype(BF16))


def _ffn_kernel(x_ref, buf_ref, nw_ref, wup_ref, bup_ref, wdw_ref, bdw_ref, wdn_ref, bdn_ref, fw_ref,
                xo_ref, bufo_ref, h_s, u_s, acc_s, *, rows, final_norm):
    F = D_FF
    first = FFN_PAD - FFN_HIST

    @pl.when(pl.program_id(1) == 0)
    def _():
        u_s[pl.ds(first, FFN_HIST), :] = buf_ref[...]

    _rmsnorm_to(x_ref, nw_ref, h_s, rows)

    h = h_s[...]
    for fb in range(F // MXU_N):
        halves = []
        for part in range(2):
            cols = slice(part * F + fb * MXU_N, part * F + (fb + 1) * MXU_N)
            u_s[pl.ds(FFN_PAD, rows), cols] = _dot(h, wup_ref[:, cols]) + bup_ref[:, cols]
            c = bdw_ref[:, cols]
            for k in range(FFN_K):
                c = c + wdw_ref[k:k + 1, cols] * u_s[pl.ds(first + k, rows), cols]
            halves.append(c)
        a, v = halves
        act = ((a * jax.nn.sigmoid(a)) * v).astype(BF16)
        contrib = _dot(act, wdn_ref[fb * MXU_N:(fb + 1) * MXU_N, :])
        if fb == 0:
            acc_s[...] = contrib
        else:
            acc_s[...] += contrib

    tail = u_s[pl.ds(rows + first, FFN_HIST), :]
    u_s[pl.ds(first, FFN_HIST), :] = tail
    bufo_ref[...] = tail

    y = x_ref[...] + (acc_s[...] + bdn_ref[...])
    if final_norm:
        y = _rmsnorm_rows(y, fw_ref[...])
    xo_ref[...] = y


def _conv_ffn(x, buf, nw, wup, bup, wdw, bdw, wdn, bdn, fw, *, rows, final_norm):
    B, S, D = x.shape
    F = D_FF
    grid = (B, S // rows)
    x_spec = pl.BlockSpec((None, rows, D), lambda b, s: (b, s, 0))
    buf_spec = pl.BlockSpec((None, FFN_HIST, 2 * F), lambda b, s: (b, 0, 0))
    return pl.pallas_call(
        functools.partial(_ffn_kernel, rows=rows, final_norm=final_norm),
        grid=grid,
        in_specs=[x_spec, buf_spec, _const_spec((1, D)), _const_spec((D, 2 * F)),
                  _const_spec((1, 2 * F)), _const_spec((FFN_K, 2 * F)), _const_spec((1, 2 * F)),
                  _const_spec((F, D)), _const_spec((1, D)), _const_spec((1, D))],
        out_specs=[x_spec, buf_spec],
        out_shape=[jax.ShapeDtypeStruct(x.shape, x.dtype),
                   jax.ShapeDtypeStruct(buf.shape, buf.dtype)],
        scratch_shapes=[pltpu.VMEM((rows, D), BF16),
                        pltpu.VMEM((FFN_PAD + rows, 2 * F), F32),
                        pltpu.VMEM((rows, D), F32)],
        compiler_params=_params(),
        name="conv_ffn",
    )(x, buf, _row2(nw), wup.astype(BF16), _row2(bup), wdw, _row2(bdw), wdn.astype(BF16),
      _row2(bdn), _row2(fw))


def _trunk(x, conv_bufs, hgrn_states, ffn_bufs, P, rows):
    depth = P['norm_mix_w'].shape[0]
    conv_new, hgrn_new, ffn_new = [], [], []
    for i in range(depth):
        j = i // 2
        if i % 2 == 0:
            x, nb = _conv_mixer(x, conv_bufs[j], P['norm_mix_w'][i], P['cv_w_pw1'][j],
                                P['cv_b_pw1'][j], P['cv_w_dw'][j], P['cv_b_dw'][j], P['cv_ln_g'][j],
                                P['cv_ln_b'][j], P['cv_w_pw2'][j], P['cv_b_pw2'][j], rows=rows)
            conv_new.append(nb)
        else:
            x, ns = _hgrn_mixer(x, hgrn_states[j], P['norm_mix_w'][i], P['hg_w_in'][j],
                                P['hg_lb_logits'], P['hg_norm_w'][j], P['hg_w_o'][j],
                                rows=rows, layer=j)
            hgrn_new.append(ns)
        x, fb = _conv_ffn(x, ffn_bufs[i], P['norm_ffn_w'][i], P['ff_w_up'][i], P['ff_b_up'][i],
                          P['ff_w_dw'][i], P['ff_b_dw'][i], P['ff_w_down'][i], P['ff_b_down'][i],
                          P['norm_final_w'], rows=rows, final_norm=(i == depth - 1))
        ffn_new.append(fb)
    return x, jnp.stack(conv_new), jnp.stack(hgrn_new), jnp.stack(ffn_new)


def kernel(x_prompt, x_sample, cache_conv, state_hgrn, cache_ffn, norm_mix_w, norm_ffn_w, norm_final_w, cv_w_pw1, cv_b_pw1, cv_w_dw, cv_b_dw, cv_ln_g, cv_ln_b, cv_w_pw2, cv_b_pw2, hg_w_in, hg_lb_logits, hg_norm_w, hg_w_o, ff_w_up, ff_b_up, ff_w_dw, ff_b_dw, ff_w_down, ff_b_down):
    P = dict(norm_mix_w=norm_mix_w, norm_ffn_w=norm_ffn_w, norm_final_w=norm_final_w,
             cv_w_pw1=cv_w_pw1, cv_b_pw1=cv_b_pw1, cv_w_dw=cv_w_dw, cv_b_dw=cv_b_dw,
             cv_ln_g=cv_ln_g, cv_ln_b=cv_ln_b, cv_w_pw2=cv_w_pw2, cv_b_pw2=cv_b_pw2,
             hg_w_in=hg_w_in, hg_lb_logits=hg_lb_logits, hg_norm_w=hg_norm_w, hg_w_o=hg_w_o,
             ff_w_up=ff_w_up, ff_b_up=ff_b_up, ff_w_dw=ff_w_dw, ff_b_dw=ff_b_dw,
             ff_w_down=ff_w_down, ff_b_down=ff_b_down)
    bp, sp, _ = x_prompt.shape
    n_conv = cache_conv.shape[0]
    n_hgrn = state_hgrn.shape[0]
    depth = cache_ffn.shape[0]
    zero_conv = jnp.zeros((n_conv, bp, CONV_HIST, D_MODEL), x_prompt.dtype)
    zero_hgrn = jnp.zeros((n_hgrn, bp, HG_HEADS, HG_DH, HG_DH), F32)
    zero_ffn = jnp.zeros((depth, bp, FFN_HIST, 2 * D_FF), x_prompt.dtype)
    y_p, conv_p, hgrn_p, ffn_p = _trunk(x_prompt, zero_conv, zero_hgrn, zero_ffn, P,
                                        min(PROMPT_TILE, sp))
    y_s, conv_s, hgrn_s, ffn_s = _trunk(x_sample, cache_conv, state_hgrn, cache_ffn, P,
                                        x_sample.shape[1])
    return (y_p, y_s, conv_p, hgrn_p, ffn_p, conv_s, hgrn_s, ffn_s)
```

```python
import functools

import jax
import jax.numpy as jnp
from jax import lax
from jax.experimental import pallas as pl
from jax.experimental.pallas import tpu as pltpu

D_MODEL = 1024
CONV_K = 31
HG_HEADS = 8
HG_DH = D_MODEL // HG_HEADS
D_FF = 2816
FFN_K = 3
EPS = 1e-6

LANES = 128
SUBLANES = 8
MXU_N = 256
VMEM_LIMIT_BYTES = 56 * 1024 * 1024

CONV_HIST = CONV_K - 1
CONV_PAD = 32
FFN_HIST = FFN_K - 1
FFN_PAD = SUBLANES
PROMPT_TILE = 512
HG_CHUNK = 32
MIN_CHUNK_LOG_DECAY = -80.0
ROW_GROUP = 64
NORM_ROWS = 32
LN_ROWS = 16
FFN_DOWN_GROUP = 4

F32 = jnp.float32
BF16 = jnp.bfloat16


def _dot(a, b):
    return jnp.dot(a, b, preferred_element_type=F32)


def _dot_nt(a, b):
    return lax.dot_general(a, b, (((1,), (1,)), ((), ())), preferred_element_type=F32)


def _dot_tn(a, b):
    return lax.dot_general(a, b, (((0,), (0,)), ((), ())), preferred_element_type=F32)


def _rmsnorm_rows(x, w):
    ms = jnp.mean(x * x, axis=-1, keepdims=True)
    return (x * lax.rsqrt(ms + EPS)) * w


def _rmsnorm_to(x_ref, w_ref, h_ref, rows):
    step = min(NORM_ROWS, rows)
    for g in range(rows // step):
        rs = pl.ds(g * step, step)
        h_ref[rs, :] = _rmsnorm_rows(x_ref[rs, :], w_ref[...]).astype(BF16)


def _conv_mixer_kernel(x_ref, hist_ref, nw_ref, w1_ref, b1_ref, wdw_ref, bdw_ref, lng_ref, lnb_ref,
                       w2_ref, b2_ref, xo_ref, histo_ref, h_s, p_s, y_s, z_s, *, rows):
    D = D_MODEL
    nlb = D // LANES
    first = CONV_PAD - CONV_HIST

    @pl.when(pl.program_id(1) == 0)
    def _():
        for lb in range(nlb):
            p_s[lb, pl.ds(first, CONV_HIST), :] = hist_ref[:, lb * LANES:(lb + 1) * LANES]

    _rmsnorm_to(x_ref, nw_ref, h_s, rows)

    h = h_s[...]
    per = MXU_N // LANES
    for cb in range(D // MXU_N):
        ca = slice(cb * MXU_N, (cb + 1) * MXU_N)
        cg = slice(D + cb * MXU_N, D + (cb + 1) * MXU_N)
        ua = _dot(h, w1_ref[:, ca]) + b1_ref[:, ca]
        ug = _dot(h, w1_ref[:, cg]) + b1_ref[:, cg]
        glu = ua * jax.nn.sigmoid(ug)
        for j in range(per):
            p_s[cb * per + j, pl.ds(CONV_PAD, rows), :] = glu[:, j * LANES:(j + 1) * LANES]

    rg = min(ROW_GROUP, rows)

    def conv_body(lb, carry):
        taps = [jnp.broadcast_to(wdw_ref[lb, k:k + 1, :], (rg, LANES)) for k in range(CONV_K)]
        bias = jnp.broadcast_to(bdw_ref[lb], (rg, LANES))
        for g in range(rows // rg):
            acc = bias
            for k in range(CONV_K):
                acc = acc + taps[k] * p_s[lb, pl.ds(g * rg + first + k, rg), :]
            y_s[lb, pl.ds(g * rg, rg), :] = acc
        return carry

    lax.fori_loop(0, nlb, conv_body, 0)

    nr = min(LN_ROWS, rows)
    for g in range(rows // nr):
        rs = pl.ds(g * nr, nr)
        y = jnp.concatenate([y_s[lb, rs, :] for lb in range(nlb)], axis=1)
        mu = jnp.mean(y, axis=-1, keepdims=True)
        d = y - mu
        var = jnp.mean(d * d, axis=-1, keepdims=True)
        yn = (d * lax.rsqrt(var + EPS)) * lng_ref[...] + lnb_ref[...]
        z_s[rs, :] = (yn * jax.nn.sigmoid(yn)).astype(BF16)

    for lb in range(nlb):
        tail = p_s[lb, pl.ds(rows + first, CONV_HIST), :]
        p_s[lb, pl.ds(first, CONV_HIST), :] = tail
        histo_ref[:, lb * LANES:(lb + 1) * LANES] = tail

    out = _dot(z_s[...], w2_ref[...]) + b2_ref[...]
    xo_ref[...] = x_ref[...] + out


def _const_spec(shape):
    zeros = (0,) * len(shape)
    return pl.BlockSpec(shape, lambda b, s: zeros, pipeline_mode=pl.Buffered(1))


def _params():
    return pltpu.CompilerParams(dimension_semantics=("arbitrary", "arbitrary"),
                                vmem_limit_bytes=VMEM_LIMIT_BYTES)


def _row2(v):
    return v.reshape(1, -1)


def _lane_blocks(v):
    r, d = v.shape
    return v.reshape(r, d // LANES, LANES).transpose(1, 0, 2)


def _conv_mixer(x, hist, nw, w1, b1, wdw, bdw, lng, lnb, w2, b2, *, rows):
    B, S, D = x.shape
    grid = (B, S // rows)
    x_spec = pl.BlockSpec((None, rows, D), lambda b, s: (b, s, 0))
    hist_spec = pl.BlockSpec((None, CONV_HIST, D), lambda b, s: (b, 0, 0))
    return pl.pallas_call(
        functools.partial(_conv_mixer_kernel, rows=rows),
        grid=grid,
        in_specs=[x_spec, hist_spec, _const_spec((1, D)), _const_spec((D, 2 * D)),
                  _const_spec((1, 2 * D)), _const_spec((D // LANES, CONV_K, LANES)),
                  _const_spec((D // LANES, 1, LANES)), _const_spec((1, D)), _const_spec((1, D)),
                  _const_spec((D, D)), _const_spec((1, D))],
        out_specs=[x_spec, hist_spec],
        out_shape=[jax.ShapeDtypeStruct(x.shape, x.dtype),
                   jax.ShapeDtypeStruct(hist.shape, hist.dtype)],
        scratch_shapes=[pltpu.VMEM((rows, D), BF16),
                        pltpu.VMEM((D // LANES, CONV_PAD + rows, LANES), F32),
                        pltpu.VMEM((D // LANES, rows, LANES), F32),
                        pltpu.VMEM((rows, D), BF16)],
        compiler_params=_params(),
        name="conv_mixer",
    )(x, hist, _row2(nw), w1.astype(BF16), _row2(b1), _lane_blocks(wdw), _lane_blocks(_row2(bdw)),
      _row2(lng), _row2(lnb), w2.astype(BF16), _row2(b2))


def _hgrn_mixer_kernel(x_ref, s0_ref, nw_ref, win_ref, lbl_ref, hnw_ref, wo_ref, xo_ref, so_ref,
                       h_s, q_s, k_s, i_s, lf_s, bc_s, g_s, o_s, z_s, st_s, *, rows, chunk, layer):
    D = D_MODEL
    nh = HG_HEADS
    dh = HG_DH

    @pl.when(pl.program_id(1) == 0)
    def _():
        for hd in range(nh):
            st_s[hd] = s0_ref[hd].T

    _rmsnorm_to(x_ref, nw_ref, h_s, rows)

    logits = lbl_ref[...]
    m = jnp.max(logits, axis=0, keepdims=True)
    e = jnp.exp(logits - m)
    p = e / jnp.sum(e, axis=0, keepdims=True)
    csum = p[0:1, :]
    for j in range(1, layer + 1):
        csum = csum + p[j:j + 1, :]
    lb = csum - p[0:1, :]

    h = h_s[...]
    for cb in range(D // MXU_N):
        cols = slice(cb * MXU_N, (cb + 1) * MXU_N)
        q_s[:, cols] = _dot(h, win_ref[:, cb * MXU_N:(cb + 1) * MXU_N])
        fl = _dot(h, win_ref[:, D + cb * MXU_N:D + (cb + 1) * MXU_N])
        f = lb[:, cols] + (1.0 - lb[:, cols]) * jax.nn.sigmoid(fl)
        lf_s[:, cols] = jnp.log(f)
        k_s[:, cols] = 1.0 - f
        i_s[:, cols] = _dot(h, win_ref[:, 2 * D + cb * MXU_N:2 * D + (cb + 1) * MXU_N]).astype(BF16)
        g_s[:, cols] = _dot(h, win_ref[:, 3 * D + cb * MXU_N:3 * D + (cb + 1) * MXU_N])

    ri = lax.broadcasted_iota(jnp.int32, (chunk, chunk), 0)
    ci = lax.broadcasted_iota(jnp.int32, (chunk, chunk), 1)
    causal = ri >= ci
    tri = causal.astype(BF16)

    mn = jnp.zeros((1, D), F32)
    for c in range(rows // chunk):
        rs = pl.ds(c * chunk, chunk)
        lf = lf_s[rs, :]
        lf_hi = lf.astype(BF16)
        lf_lo = (lf - lf_hi.astype(F32)).astype(BF16)
        bc = _dot(tri, lf_hi) + _dot(tri, lf_lo)
        bc_s[rs, :] = bc
        mn = jnp.minimum(mn, bc[chunk - 1:chunk, :])
    safe = jnp.min(mn) >= MIN_CHUNK_LOG_DECAY

    def chunk_body(c, carry):
        r = pl.multiple_of(c * chunk, chunk)
        bc = bc_s[pl.ds(r, chunk), :]
        b_last = bc[chunk - 1:chunk, :]
        qd = (q_s[pl.ds(r, chunk), :] * jnp.exp(bc)).astype(BF16)
        kk = k_s[pl.ds(r, chunk), :]
        kd = (kk * jnp.exp(-bc)).astype(BF16)
        ke = (kk * jnp.exp(b_last - bc)).astype(BF16)
        dec = jnp.exp(b_last)
        iv = i_s[pl.ds(r, chunk), :]
        for hd in range(nh):
            cols = slice(hd * dh, (hd + 1) * dh)
            st = st_s[hd]
            sc = jnp.where(causal, _dot_nt(qd[:, cols], kd[:, cols]), 0.0)
            o = _dot(sc.astype(BF16), iv[:, cols]) + _dot_nt(qd[:, cols], st.astype(BF16))
            o_s[pl.ds(r, chunk), cols] = o
            st_s[hd] = st * dec[:, cols] + _dot_tn(iv[:, cols], ke[:, cols])
        return carry

    row_id = lax.broadcasted_iota(jnp.int32, (chunk, 1), 0)

    def slow_chunk_body(c, carry):
        r = pl.multiple_of(c * chunk, chunk)
        lf = lf_s[pl.ds(r, chunk), :]
        qc = q_s[pl.ds(r, chunk), :].astype(BF16)
        kc = k_s[pl.ds(r, chunk), :].astype(BF16)
        iv = i_s[pl.ds(r, chunk), :]
        o_s[pl.ds(r, chunk), :] = jnp.zeros((chunk, D), F32)

        def token_body(t, carry2):
            sel = row_id == t
            f_row = jnp.exp(jnp.sum(jnp.where(sel, lf, 0.0), axis=0, keepdims=True))
            i_sel = jnp.where(sel, iv, jnp.zeros_like(iv))
            for hd in range(nh):
                cols = slice(hd * dh, (hd + 1) * dh)
                st = st_s[hd] * f_row[:, cols] + _dot_tn(i_sel[:, cols], kc[:, cols])
                st_s[hd] = st
                res = _dot_nt(qc[:, cols], st.astype(BF16))
                o_s[pl.ds(r, chunk), cols] = jnp.where(sel, res, o_s[pl.ds(r, chunk), cols])
            return carry2

        lax.fori_loop(0, chunk, token_body, 0)
        return carry

    @pl.when(safe)
    def _():
        lax.fori_loop(0, rows // chunk, chunk_body, 0, unroll=min(2, rows // chunk))

    @pl.when(jnp.logical_not(safe))
    def _():
        lax.fori_loop(0, rows // chunk, slow_chunk_body, 0)

    nr = min(NORM_ROWS, rows)
    for g in range(rows // nr):
        rs = pl.ds(g * nr, nr)
        for hd in range(nh):
            cols = slice(hd * dh, (hd + 1) * dh)
            o = o_s[rs, cols]
            ms = jnp.mean(o * o, axis=-1, keepdims=True)
            gt = g_s[rs, cols]
            on = (o * lax.rsqrt(ms + EPS)) * hnw_ref[:, cols] * (gt * jax.nn.sigmoid(gt))
            z_s[rs, cols] = on.astype(BF16)

    xo_ref[...] = x_ref[...] + _dot(z_s[...], wo_ref[...])

    @pl.when(pl.program_id(1) == pl.num_programs(1) - 1)
    def _():
        for hd in range(nh):
            so_ref[hd] = st_s[hd].T


def _hgrn_mixer(x, s0, nw, win, lb_logits, hnw, wo, *, rows, layer):
    B, S, D = x.shape
    chunk = min(HG_CHUNK, rows)
    grid = (B, S // rows)
    x_spec = pl.BlockSpec((None, rows, D), lambda b, s: (b, s, 0))
    st_spec = pl.BlockSpec((None, HG_HEADS, HG_DH, HG_DH), lambda b, s: (b, 0, 0, 0))
    n_layers = lb_logits.shape[0]
    return pl.pallas_call(
        functools.partial(_hgrn_mixer_kernel, rows=rows, chunk=chunk, layer=layer),
        grid=grid,
        in_specs=[x_spec, st_spec, _const_spec((1, D)), _const_spec((D, 4 * D)),
                  _const_spec((n_layers, D)), _const_spec((1, D)), _const_spec((D, D))],
        out_specs=[x_spec, st_spec],
        out_shape=[jax.ShapeDtypeStruct(x.shape, x.dtype),
                   jax.ShapeDtypeStruct(s0.shape, s0.dtype)],
        scratch_shapes=[pltpu.VMEM((rows, D), BF16),
                        pltpu.VMEM((rows, D), F32),
                        pltpu.VMEM((rows, D), F32),
                        pltpu.VMEM((rows, D), BF16),
                        pltpu.VMEM((rows, D), F32),
                        pltpu.VMEM((rows, D), F32),
                        pltpu.VMEM((rows, D), F32),
                        pltpu.VMEM((rows, D), F32),
                        pltpu.VMEM((rows, D), BF16),
                        pltpu.VMEM((HG_HEADS, HG_DH, HG_DH), F32)],
        compiler_params=_params(),
        name="hgrn_mixer",
    )(x, s0, _row2(nw), win.astype(BF16), lb_logits, _row2(hnw), wo.astype(BF16))


def _ffn_kernel(x_ref, buf_ref, nw_ref, wup_ref, bup_ref, wdw_ref, bdw_ref, wdn_ref, bdn_ref, fw_ref,
                xo_ref, bufo_ref, h_s, u_s, acc_s, *, rows, final_norm):
    F = D_FF
    first = FFN_PAD - FFN_HIST
    nlb = 2 * F // LANES
    per = MXU_N // LANES
    nfb = F // MXU_N

    @pl.when(pl.program_id(1) == 0)
    def _():
        for lb in range(nlb):
            u_s[lb, pl.ds(first, FFN_HIST), :] = buf_ref[:, lb * LANES:(lb + 1) * LANES]

    _rmsnorm_to(x_ref, nw_ref, h_s, rows)

    def conv_piece(lb):
        cols = slice(lb * LANES, (lb + 1) * LANES)
        c = bdw_ref[:, cols]
        for k in range(FFN_K):
            c = c + wdw_ref[k:k + 1, cols] * u_s[lb, pl.ds(first + k, rows), :]
        return c

    h = h_s[...]
    fb = 0
    while fb < nfb:
        group = min(FFN_DOWN_GROUP, nfb - fb)
        acts = []
        for b in range(fb, fb + group):
            for part in range(2):
                cols = slice(part * F + b * MXU_N, part * F + (b + 1) * MXU_N)
                u = _dot(h, wup_ref[:, cols]) + bup_ref[:, cols]
                for j in range(per):
                    lb = (part * F + b * MXU_N) // LANES + j
                    u_s[lb, pl.ds(FFN_PAD, rows), :] = u[:, j * LANES:(j + 1) * LANES]
            for j in range(per):
                a = conv_piece(b * per + j)
                v = conv_piece(F // LANES + b * per + j)
                acts.append(((a * jax.nn.sigmoid(a)) * v).astype(BF16))
        contrib = _dot(jnp.concatenate(acts, axis=1), wdn_ref[fb * MXU_N:(fb + group) * MXU_N, :])
        if fb == 0:
            acc_s[...] = contrib
        else:
            acc_s[...] += contrib
        fb += group

    for lb in range(nlb):
        tail = u_s[lb, pl.ds(rows + first, FFN_HIST), :]
        u_s[lb, pl.ds(first, FFN_HIST), :] = tail
        bufo_ref[:, lb * LANES:(lb + 1) * LANES] = tail

    step = min(NORM_ROWS, rows)
    for g in range(rows // step):
        rs = pl.ds(g * step, step)
        y = x_ref[rs, :] + (acc_s[rs, :] + bdn_ref[...])
        if final_norm:
            y = _rmsnorm_rows(y, fw_ref[...])
        xo_ref[rs, :] = y


def _conv_ffn(x, buf, nw, wup, bup, wdw, bdw, wdn, bdn, fw, *, rows, final_norm):
    B, S, D = x.shape
    F = D_FF
    grid = (B, S // rows)
    x_spec = pl.BlockSpec((None, rows, D), lambda b, s: (b, s, 0))
    buf_spec = pl.BlockSpec((None, FFN_HIST, 2 * F), lambda b, s: (b, 0, 0))
    return pl.pallas_call(
        functools.partial(_ffn_kernel, rows=rows, final_norm=final_norm),
        grid=grid,
        in_specs=[x_spec, buf_spec, _const_spec((1, D)), _const_spec((D, 2 * F)),
                  _const_spec((1, 2 * F)), _const_spec((FFN_K, 2 * F)), _const_spec((1, 2 * F)),
                  _const_spec((F, D)), _const_spec((1, D)), _const_spec((1, D))],
        out_specs=[x_spec, buf_spec],
        out_shape=[jax.ShapeDtypeStruct(x.shape, x.dtype),
                   jax.ShapeDtypeStruct(buf.shape, buf.dtype)],
        scratch_shapes=[pltpu.VMEM((rows, D), BF16),
                        pltpu.VMEM((2 * F // LANES, FFN_PAD + rows, LANES), F32),
                        pltpu.VMEM((rows, D), F32)],
        compiler_params=_params(),
        name="conv_ffn",
    )(x, buf, _row2(nw), wup.astype(BF16), _row2(bup), wdw, _row2(bdw), wdn.astype(BF16),
      _row2(bdn), _row2(fw))


def _trunk(x, conv_bufs, hgrn_states, ffn_bufs, P, rows):
    depth = P['norm_mix_w'].shape[0]
    conv_new, hgrn_new, ffn_new = [], [], []
    for i in range(depth):
        j = i // 2
        if i % 2 == 0:
            x, nb = _conv_mixer(x, conv_bufs[j], P['norm_mix_w'][i], P['cv_w_pw1'][j],
                                P['cv_b_pw1'][j], P['cv_w_dw'][j], P['cv_b_dw'][j], P['cv_ln_g'][j],
                                P['cv_ln_b'][j], P['cv_w_pw2'][j], P['cv_b_pw2'][j], rows=rows)
            conv_new.append(nb)
        else:
            x, ns = _hgrn_mixer(x, hgrn_states[j], P['norm_mix_w'][i], P['hg_w_in'][j],
                                P['hg_lb_logits'], P['hg_norm_w'][j], P['hg_w_o'][j],
                                rows=rows, layer=j)
            hgrn_new.append(ns)
        x, fb = _conv_ffn(x, ffn_bufs[i], P['norm_ffn_w'][i], P['ff_w_up'][i], P['ff_b_up'][i],
                          P['ff_w_dw'][i], P['ff_b_dw'][i], P['ff_w_down'][i], P['ff_b_down'][i],
                          P['norm_final_w'], rows=rows, final_norm=(i == depth - 1))
        ffn_new.append(fb)
    return x, jnp.stack(conv_new), jnp.stack(hgrn_new), jnp.stack(ffn_new)


def kernel(x_prompt, x_sample, cache_conv, state_hgrn, cache_ffn, norm_mix_w, norm_ffn_w, norm_final_w, cv_w_pw1, cv_b_pw1, cv_w_dw, cv_b_dw, cv_ln_g, cv_ln_b, cv_w_pw2, cv_b_pw2, hg_w_in, hg_lb_logits, hg_norm_w, hg_w_o, ff_w_up, ff_b_up, ff_w_dw, ff_b_dw, ff_w_down, ff_b_down):
    P = dict(norm_mix_w=norm_mix_w, norm_ffn_w=norm_ffn_w, norm_final_w=norm_final_w,
             cv_w_pw1=cv_w_pw1, cv_b_pw1=cv_b_pw1, cv_w_dw=cv_w_dw, cv_b_dw=cv_b_dw,
             cv_ln_g=cv_ln_g, cv_ln_b=cv_ln_b, cv_w_pw2=cv_w_pw2, cv_b_pw2=cv_b_pw2,
             hg_w_in=hg_w_in, hg_lb_logits=hg_lb_logits, hg_norm_w=hg_norm_w, hg_w_o=hg_w_o,
             ff_w_up=ff_w_up, ff_b_up=ff_b_up, ff_w_dw=ff_w_dw, ff_b_dw=ff_b_dw,
             ff_w_down=ff_w_down, ff_b_down=ff_b_down)
    bp, sp, _ = x_prompt.shape
    n_conv = cache_conv.shape[0]
    n_hgrn = state_hgrn.shape[0]
    depth = cache_ffn.shape[0]
    zero_conv = jnp.zeros((n_conv, bp, CONV_HIST, D_MODEL), x_prompt.dtype)
    zero_hgrn = jnp.zeros((n_hgrn, bp, HG_HEADS, HG_DH, HG_DH), F32)
    zero_ffn = jnp.zeros((depth, bp, FFN_HIST, 2 * D_FF), x_prompt.dtype)
    y_p, conv_p, hgrn_p, ffn_p = _trunk(x_prompt, zero_conv, zero_hgrn, zero_ffn, P,
                                        min(PROMPT_TILE, sp))
    y_s, conv_s, hgrn_s, ffn_s = _trunk(x_sample, cache_conv, state_hgrn, cache_ffn, P,
                                        x_sample.shape[1])
    return (y_p, y_s, conv_p, hgrn_p, ffn_p, conv_s, hgrn_s, ffn_s)
```

```python
import functools

import jax
import jax.numpy as jnp
from jax import lax
from jax.experimental import pallas as pl
from jax.experimental.pallas import tpu as pltpu

D_MODEL = 1024
CONV_K = 31
HG_HEADS = 8
HG_DH = D_MODEL // HG_HEADS
D_FF = 2816
FFN_K = 3
EPS = 1e-6

LANES = 128
SUBLANES = 8
MXU_N = 256
VMEM_LIMIT_BYTES = 56 * 1024 * 1024

CONV_HIST = CONV_K - 1
CONV_PAD = 32
FFN_HIST = FFN_K - 1
FFN_PAD = SUBLANES
PROMPT_TILE = 512
HG_CHUNK = 32
MIN_CHUNK_LOG_DECAY = -80.0
ROW_GROUP = 64
NORM_ROWS = 32
LN_ROWS = 16
FFN_DOWN_GROUP = 4

F32 = jnp.float32
BF16 = jnp.bfloat16


def _dot(a, b):
    return jnp.dot(a, b, preferred_element_type=F32)


def _dot_nt(a, b):
    return lax.dot_general(a, b, (((1,), (1,)), ((), ())), preferred_element_type=F32)


def _dot_tn(a, b):
    return lax.dot_general(a, b, (((0,), (0,)), ((), ())), preferred_element_type=F32)


def _rmsnorm_rows(x, w):
    ms = jnp.mean(x * x, axis=-1, keepdims=True)
    return (x * lax.rsqrt(ms + EPS)) * w


def _rmsnorm_to(x_ref, w_ref, h_ref, rows):
    step = min(NORM_ROWS, rows)
    for g in range(rows // step):
        rs = pl.ds(g * step, step)
        h_ref[rs, :] = _rmsnorm_rows(x_ref[rs, :], w_ref[...]).astype(BF16)


def _conv_mixer_kernel(x_ref, hist_ref, nw_ref, w1_ref, b1_ref, wdw_ref, bdw_ref, lng_ref, lnb_ref,
                       w2_ref, b2_ref, xo_ref, histo_ref, h_s, p_s, y_s, z_s, *, rows):
    D = D_MODEL
    nlb = D // LANES
    first = CONV_PAD - CONV_HIST

    @pl.when(pl.program_id(1) == 0)
    def _():
        for lb in range(nlb):
            p_s[lb, pl.ds(first, CONV_HIST), :] = hist_ref[:, lb * LANES:(lb + 1) * LANES]

    _rmsnorm_to(x_ref, nw_ref, h_s, rows)

    h = h_s[...]
    per = MXU_N // LANES
    for cb in range(D // MXU_N):
        ca = slice(cb * MXU_N, (cb + 1) * MXU_N)
        cg = slice(D + cb * MXU_N, D + (cb + 1) * MXU_N)
        ua = _dot(h, w1_ref[:, ca]) + b1_ref[:, ca]
        ug = _dot(h, w1_ref[:, cg]) + b1_ref[:, cg]
        glu = ua * jax.nn.sigmoid(ug)
        for j in range(per):
            p_s[cb * per + j, pl.ds(CONV_PAD, rows), :] = glu[:, j * LANES:(j + 1) * LANES]

    rg = min(ROW_GROUP, rows)

    def conv_body(lb, carry):
        taps = [jnp.broadcast_to(wdw_ref[lb, k:k + 1, :], (rg, LANES)) for k in range(CONV_K)]
        bias = jnp.broadcast_to(bdw_ref[lb], (rg, LANES))
        for g in range(rows // rg):
            acc = bias
            for k in range(CONV_K):
                acc = acc + taps[k] * p_s[lb, pl.ds(g * rg + first + k, rg), :]
            y_s[lb, pl.ds(g * rg, rg), :] = acc
        return carry

    lax.fori_loop(0, nlb, conv_body, 0)

    nr = min(LN_ROWS, rows)
    for g in range(rows // nr):
        rs = pl.ds(g * nr, nr)
        y = jnp.concatenate([y_s[lb, rs, :] for lb in range(nlb)], axis=1)
        mu = jnp.mean(y, axis=-1, keepdims=True)
        d = y - mu
        var = jnp.mean(d * d, axis=-1, keepdims=True)
        yn = (d * lax.rsqrt(var + EPS)) * lng_ref[...] + lnb_ref[...]
        z_s[rs, :] = (yn * jax.nn.sigmoid(yn)).astype(BF16)

    for lb in range(nlb):
        tail = p_s[lb, pl.ds(rows + first, CONV_HIST), :]
        p_s[lb, pl.ds(first, CONV_HIST), :] = tail
        histo_ref[:, lb * LANES:(lb + 1) * LANES] = tail

    out = _dot(z_s[...], w2_ref[...]) + b2_ref[...]
    xo_ref[...] = x_ref[...] + out


def _const_spec(shape):
    zeros = (0,) * len(shape)
    return pl.BlockSpec(shape, lambda b, s: zeros, pipeline_mode=pl.Buffered(1))


def _params():
    return pltpu.CompilerParams(dimension_semantics=("arbitrary", "arbitrary"),
                                vmem_limit_bytes=VMEM_LIMIT_BYTES)


def _row2(v):
    return v.reshape(1, -1)


def _lane_blocks(v):
    r, d = v.shape
    return v.reshape(r, d // LANES, LANES).transpose(1, 0, 2)


def _conv_mixer(x, hist, nw, w1, b1, wdw, bdw, lng, lnb, w2, b2, *, rows):
    B, S, D = x.shape
    grid = (B, S // rows)
    x_spec = pl.BlockSpec((None, rows, D), lambda b, s: (b, s, 0))
    hist_spec = pl.BlockSpec((None, CONV_HIST, D), lambda b, s: (b, 0, 0))
    return pl.pallas_call(
        functools.partial(_conv_mixer_kernel, rows=rows),
        grid=grid,
        in_specs=[x_spec, hist_spec, _const_spec((1, D)), _const_spec((D, 2 * D)),
                  _const_spec((1, 2 * D)), _const_spec((D // LANES, CONV_K, LANES)),
                  _const_spec((D // LANES, 1, LANES)), _const_spec((1, D)), _const_spec((1, D)),
                  _const_spec((D, D)), _const_spec((1, D))],
        out_specs=[x_spec, hist_spec],
        out_shape=[jax.ShapeDtypeStruct(x.shape, x.dtype),
                   jax.ShapeDtypeStruct(hist.shape, hist.dtype)],
        scratch_shapes=[pltpu.VMEM((rows, D), BF16),
                        pltpu.VMEM((D // LANES, CONV_PAD + rows, LANES), F32),
                        pltpu.VMEM((D // LANES, rows, LANES), F32),
                        pltpu.VMEM((rows, D), BF16)],
        compiler_params=_params(),
        name="conv_mixer",
    )(x, hist, _row2(nw), w1.astype(BF16), _row2(b1), _lane_blocks(wdw), _lane_blocks(_row2(bdw)),
      _row2(lng), _row2(lnb), w2.astype(BF16), _row2(b2))


def _hgrn_mixer_kernel(x_ref, s0_ref, nw_ref, win_ref, lbl_ref, hnw_ref, wo_ref, xo_ref, so_ref,
                       h_s, q_s, k_s, i_s, lf_s, bc_s, g_s, o_s, z_s, st_s, sb_s, *, rows, chunk, layer):
    D = D_MODEL
    nh = HG_HEADS
    dh = HG_DH

    @pl.when(pl.program_id(1) == 0)
    def _():
        for hd in range(nh):
            st_s[hd] = s0_ref[hd].T
            sb_s[hd] = s0_ref[hd].astype(BF16)

    _rmsnorm_to(x_ref, nw_ref, h_s, rows)

    logits = lbl_ref[...]
    m = jnp.max(logits, axis=0, keepdims=True)
    e = jnp.exp(logits - m)
    p = e / jnp.sum(e, axis=0, keepdims=True)
    csum = p[0:1, :]
    for j in range(1, layer + 1):
        csum = csum + p[j:j + 1, :]
    lb = csum - p[0:1, :]

    h = h_s[...]
    for cb in range(D // MXU_N):
        cols = slice(cb * MXU_N, (cb + 1) * MXU_N)
        q_s[:, cols] = _dot(h, win_ref[:, cb * MXU_N:(cb + 1) * MXU_N])
        fl = _dot(h, win_ref[:, D + cb * MXU_N:D + (cb + 1) * MXU_N])
        f = lb[:, cols] + (1.0 - lb[:, cols]) * jax.nn.sigmoid(fl)
        lf_s[:, cols] = jnp.log(f)
        k_s[:, cols] = 1.0 - f
        i_s[:, cols] = _dot(h, win_ref[:, 2 * D + cb * MXU_N:2 * D + (cb + 1) * MXU_N]).astype(BF16)
        g_s[:, cols] = _dot(h, win_ref[:, 3 * D + cb * MXU_N:3 * D + (cb + 1) * MXU_N])

    ri =lax.broadcasted_iota(jnp.int32, (chunk, chunk), 0)
    ci = lax.broadcasted_iota(jnp.int32, (chunk, chunk), 1)
    causal = ri >= ci
    tri = causal.astype(BF16)

    mn = jnp.zeros((1, D), F32)
    for c in range(rows // chunk):
        rs = pl.ds(c * chunk, chunk)
        lf = lf_s[rs, :]
        lf_hi = lf.astype(BF16)
        lf_lo = (lf - lf_hi.astype(F32)).astype(BF16)
        bc = _dot(tri, lf_hi) + _dot(tri, lf_lo)
        bc_s[rs, :] = bc
        mn = jnp.minimum(mn, bc[chunk - 1:chunk, :])
    safe = jnp.min(mn) >= MIN_CHUNK_LOG_DECAY

    def chunk_step(c):
        rs = pl.ds(c * chunk, chunk)
        bc = bc_s[rs, :]
        b_last = bc[chunk - 1:chunk, :]
        qd = (q_s[rs, :] * jnp.exp(bc)).astype(BF16)
        kk = k_s[rs, :]
        kd = (kk * jnp.exp(-bc)).astype(BF16)
        ke = (kk * jnp.exp(b_last - bc)).astype(BF16)
        dec = jnp.exp(b_last)
        iv = i_s[rs, :]
        for hd in range(nh):
            cols = slice(hd * dh, (hd + 1) * dh)
            sc = jnp.where(causal, _dot_nt(qd[:, cols], kd[:, cols]), 0.0)
            o_s[rs, cols] = _dot(sc.astype(BF16), iv[:, cols]) + _dot(qd[:, cols], sb_s[hd])
            st = st_s[hd] * dec[:, cols] + _dot_tn(iv[:, cols], ke[:, cols])
            st_s[hd] = st
            sb_s[hd] = st.astype(BF16).T

    row_id = lax.broadcasted_iota(jnp.int32, (chunk, 1), 0)

    def slow_chunk_body(c, carry):
        r = pl.multiple_of(c * chunk, chunk)
        lf = lf_s[pl.ds(r, chunk), :]
        qc = q_s[pl.ds(r, chunk), :].astype(BF16)
        kc = k_s[pl.ds(r, chunk), :].astype(BF16)
        iv = i_s[pl.ds(r, chunk), :]
        o_s[pl.ds(r, chunk), :] = jnp.zeros((chunk, D), F32)

        def token_body(t, carry2):
            sel = row_id == t
            f_row = jnp.exp(jnp.sum(jnp.where(sel, lf, 0.0), axis=0, keepdims=True))
            i_sel = jnp.where(sel, iv, jnp.zeros_like(iv))
            for hd in range(nh):
                cols = slice(hd * dh, (hd + 1) * dh)
                st = st_s[hd] * f_row[:, cols] + _dot_tn(i_sel[:, cols], kc[:, cols])
                st_s[hd] = st
                res = _dot_nt(qc[:, cols], st.astype(BF16))
                o_s[pl.ds(r, chunk), cols] = jnp.where(sel, res, o_s[pl.ds(r, chunk), cols])
            return carry2

        lax.fori_loop(0, chunk, token_body, 0)
        return carry

    @pl.when(safe)
    def _():
        for c in range(rows // chunk):
            chunk_step(c)

    @pl.when(jnp.logical_not(safe))
    def _():
        lax.fori_loop(0, rows // chunk, slow_chunk_body, 0)
        for hd in range(nh):
            sb_s[hd] = st_s[hd].astype(BF16).T

    nr = min(NORM_ROWS, rows)
    for g in range(rows // nr):
        rs = pl.ds(g * nr, nr)
        for hd in range(nh):
            cols = slice(hd * dh, (hd + 1) * dh)
            o = o_s[rs, cols]
            ms = jnp.mean(o * o, axis=-1, keepdims=True)
            gt = g_s[rs, cols]
            on = (o * lax.rsqrt(ms + EPS)) * hnw_ref[:, cols] * (gt * jax.nn.sigmoid(gt))
            z_s[rs, cols] = on.astype(BF16)

    xo_ref[...] = x_ref[...] + _dot(z_s[...], wo_ref[...])

    @pl.when(pl.program_id(1) == pl.num_programs(1) - 1)
    def _():
        for hd in range(nh):
            so_ref[hd] = st_s[hd].T


def _hgrn_mixer(x, s0, nw, win, lb_logits, hnw, wo, *, rows, layer):
    B, S, D = x.shape
    chunk = min(HG_CHUNK, rows)
    grid = (B, S // rows)
    x_spec = pl.BlockSpec((None, rows, D), lambda b, s: (b, s, 0))
    st_spec = pl.BlockSpec((None, HG_HEADS, HG_DH, HG_DH), lambda b, s: (b, 0, 0, 0))
    n_layers = lb_logits.shape[0]
    return pl.pallas_call(
        functools.partial(_hgrn_mixer_kernel, rows=rows, chunk=chunk, layer=layer),
        grid=grid,
        in_specs=[x_spec, st_spec, _const_spec((1, D)), _const_spec((D, 4 * D)),
                  _const_spec((n_layers, D)), _const_spec((1, D)), _const_spec((D, D))],
        out_specs=[x_spec, st_spec],
        out_shape=[jax.ShapeDtypeStruct(x.shape, x.dtype),
                   jax.ShapeDtypeStruct(s0.shape, s0.dtype)],
        scratch_shapes=[pltpu.VMEM((rows, D), BF16),
                        pltpu.VMEM((rows, D), F32),
                        pltpu.VMEM((rows, D), F32),
                        pltpu.VMEM((rows, D), BF16),
                        pltpu.VMEM((rows, D), F32),
                        pltpu.VMEM((rows, D), F32),
                        pltpu.VMEM((rows, D), F32),
                        pltpu.VMEM((rows, D), F32),
                        pltpu.VMEM((rows, D), BF16),
                        pltpu.VMEM((HG_HEADS, HG_DH, HG_DH), F32),
                        pltpu.VMEM((HG_HEADS, HG_DH, HG_DH), BF16)],
        compiler_params=_params(),
        name="hgrn_mixer",
    )(x, s0, _row2(nw), win.astype(BF16), lb_logits, _row2(hnw), wo.astype(BF16))


def _ffn_kernel(x_ref, buf_ref, nw_ref, wup_ref, bup_ref, wdw_ref, bdw_ref, wdn_ref, bdn_ref, fw_ref,
                xo_ref, bufo_ref, h_s, u_s, acc_s, *, rows, final_norm):
    F = D_FF
    first = FFN_PAD - FFN_HIST
    nlb = 2 * F // LANES
    per = MXU_N // LANES
    nfb = F // MXU_N

    @pl.when(pl.program_id(1) == 0)
    def _():
        for lb in range(nlb):
            u_s[lb, pl.ds(first, FFN_HIST), :] = buf_ref[:, lb * LANES:(lb + 1) * LANES]

    _rmsnorm_to(x_ref, nw_ref, h_s, rows)

    def conv_piece(lb):
        cols = slice(lb * LANES, (lb + 1) * LANES)
        c = bdw_ref[:, cols]
        for k in range(FFN_K):
            c = c + wdw_ref[k:k + 1, cols] * u_s[lb, pl.ds(first + k, rows), :]
        return c

    h = h_s[...]

    def up_proj(b):
        for part in range(2):
            cols = slice(part * F + b * MXU_N, part * F + (b + 1) * MXU_N)
            u = _dot(h, wup_ref[:, cols]) + bup_ref[:, cols]
            for j in range(per):
                lb = (part * F + b * MXU_N) // LANES + j
                u_s[lb, pl.ds(FFN_PAD, rows), :] = u[:, j * LANES:(j + 1) * LANES]

    def act_block(b):
        out = []
        for j in range(per):
            a = conv_piece(b * per + j)
            v = conv_piece(F // LANES + b * per + j)
            out.append(((a * jax.nn.sigmoid(a)) * v).astype(BF16))
        return out

    groups = [(fb, min(FFN_DOWN_GROUP, nfb - fb)) for fb in range(0, nfb, FFN_DOWN_GROUP)]

    for b in range(groups[0][1]):
        up_proj(b)
    for gi, (fb, group) in enumerate(groups):
        if gi + 1 < len(groups):
            nxt, n_nxt = groups[gi + 1]
            for b in range(nxt, nxt + n_nxt):
                up_proj(b)
        acts = []
        for b in range(fb, fb + group):
            acts += act_block(b)
        contrib = _dot(jnp.concatenate(acts, axis=1), wdn_ref[fb * MXU_N:(fb + group) * MXU_N, :])
        if gi == 0:
            acc_s[...] = contrib
        else:
            acc_s[...] += contrib

    for lb in range(nlb):
        tail = u_s[lb, pl.ds(rows + first, FFN_HIST), :]
        u_s[lb, pl.ds(first, FFN_HIST), :] = tail
        bufo_ref[:, lb * LANES:(lb + 1) * LANES] = tail

    step = min(NORM_ROWS, rows)
    for g in range(rows // step):
        rs = pl.ds(g * step, step)
        y = x_ref[rs, :] + (acc_s[rs, :] + bdn_ref[...])
        if final_norm:
            y = _rmsnorm_rows(y, fw_ref[...])
        xo_ref[rs, :] = y


def _conv_ffn(x, buf, nw, wup, bup, wdw, bdw, wdn, bdn, fw, *, rows, final_norm):
    B, S, D = x.shape
    F = D_FF
    grid = (B, S // rows)
    x_spec = pl.BlockSpec((None, rows, D), lambda b, s: (b, s, 0))
    buf_spec = pl.BlockSpec((None, FFN_HIST, 2 * F), lambda b, s: (b, 0, 0))
    return pl.pallas_call(
        functools.partial(_ffn_kernel, rows=rows, final_norm=final_norm),
        grid=grid,
        in_specs=[x_spec, buf_spec, _const_spec((1, D)), _const_spec((D, 2 * F)),
                  _const_spec((1, 2 * F)), _const_spec((FFN_K, 2 * F)), _const_spec((1, 2 * F)),
                  _const_spec((F, D)), _const_spec((1, D)), _const_spec((1, D))],
        out_specs=[x_spec, buf_spec],
        out_shape=[jax.ShapeDtypeStruct(x.shape, x.dtype),
                   jax.ShapeDtypeStruct(buf.shape, buf.dtype)],
        scratch_shapes=[pltpu.VMEM((rows, D), BF16),
                        pltpu.VMEM((2 * F // LANES, FFN_PAD + rows, LANES), F32),
                        pltpu.VMEM((rows, D), F32)],
        compiler_params=_params(),
        name="conv_ffn",
    )(x, buf, _row2(nw), wup.astype(BF16), _row2(bup), wdw, _row2(bdw), wdn.astype(BF16),
      _row2(bdn), _row2(fw))


def _trunk(x, conv_bufs, hgrn_states, ffn_bufs, P, rows):
    depth = P['norm_mix_w'].shape[0]
    conv_new, hgrn_new, ffn_new = [], [], []
    for i in range(depth):
        j = i // 2
        if i % 2 == 0:
            x, nb = _conv_mixer(x, conv_bufs[j], P['norm_mix_w'][i], P['cv_w_pw1'][j],
                                P['cv_b_pw1'][j], P['cv_w_dw'][j], P['cv_b_dw'][j], P['cv_ln_g'][j],
                                P['cv_ln_b'][j], P['cv_w_pw2'][j], P['cv_b_pw2'][j], rows=rows)
            conv_new.append(nb)
        else:
            x, ns = _hgrn_mixer(x, hgrn_states[j], P['norm_mix_w'][i], P['hg_w_in'][j],
                                P['hg_lb_logits'], P['hg_norm_w'][j], P['hg_w_o'][j],
                                rows=rows, layer=j)
            hgrn_new.append(ns)
        x, fb = _conv_ffn(x, ffn_bufs[i], P['norm_ffn_w'][i], P['ff_w_up'][i], P['ff_b_up'][i],
                          P['ff_w_dw'][i], P['ff_b_dw'][i], P['ff_w_down'][i], P['ff_b_down'][i],
                          P['norm_final_w'], rows=rows, final_norm=(i == depth - 1))
        ffn_new.append(fb)
    return x, jnp.stack(conv_new), jnp.stack(hgrn_new), jnp.stack(ffn_new)


def kernel(x_prompt, x_sample, cache_conv, state_hgrn, cache_ffn, norm_mix_w, norm_ffn_w, norm_final_w, cv_w_pw1, cv_b_pw1, cv_w_dw, cv_b_dw, cv_ln_g, cv_ln_b, cv_w_pw2, cv_b_pw2, hg_w_in, hg_lb_logits, hg_norm_w, hg_w_o, ff_w_up, ff_b_up, ff_w_dw, ff_b_dw, ff_w_down, ff_b_down):
    P = dict(norm_mix_w=norm_mix_w, norm_ffn_w=norm_ffn_w, norm_final_w=norm_final_w,
             cv_w_pw1=cv_w_pw1, cv_b_pw1=cv_b_pw1, cv_w_dw=cv_w_dw, cv_b_dw=cv_b_dw,
             cv_ln_g=cv_ln_g, cv_ln_b=cv_ln_b, cv_w_pw2=cv_w_pw2, cv_b_pw2=cv_b_pw2,
             hg_w_in=hg_w_in, hg_lb_logits=hg_lb_logits, hg_norm_w=hg_norm_w, hg_w_o=hg_w_o,
             ff_w_up=ff_w_up, ff_b_up=ff_b_up, ff_w_dw=ff_w_dw, ff_b_dw=ff_b_dw,
             ff_w_down=ff_w_down, ff_b_down=ff_b_down)
    bp, sp, _ = x_prompt.shape
    n_conv = cache_conv.shape[0]
    n_hgrn = state_hgrn.shape[0]
    depth = cache_ffn.shape[0]
    zero_conv = jnp.zeros((n_conv, bp, CONV_HIST, D_MODEL), x_prompt.dtype)
    zero_hgrn = jnp.zeros((n_hgrn, bp, HG_HEADS, HG_DH, HG_DH), F32)
    zero_ffn = jnp.zeros((depth, bp, FFN_HIST, 2 * D_FF), x_prompt.dtype)
    y_p, conv_p, hgrn_p, ffn_p = _trunk(x_prompt, zero_conv, zero_hgrn, zero_ffn, P,
                                        min(PROMPT_TILE, sp))
    y_s, conv_s, hgrn_s, ffn_s = _trunk(x_sample, cache_conv, state_hgrn, cache_ffn, P,
                                        x_sample.shape[1])
    return (y_p, y_s, conv_p, hgrn_p, ffn_p, conv_s, hgrn_s, ffn_s)
```

```python
import functools

import jax
import jax.numpy as jnp
from jax import lax
from jax.experimental import pallas as pl
from jax.experimental.pallas import tpu as pltpu

D_MODEL = 1024
CONV_K = 31
HG_HEADS = 8
HG_DH = D_MODEL // HG_HEADS
D_FF = 2816
FFN_K = 3
EPS = 1e-6

LANES = 128
SUBLANES = 8
MXU_N = 256
VMEM_LIMIT_BYTES = 56 * 1024 * 1024

CONV_HIST = CONV_K - 1
CONV_PAD = 32
FFN_HIST = FFN_K - 1
FFN_PAD = SUBLANES
PROMPT_TILE = 512
HG_CHUNK = 32
MIN_CHUNK_LOG_DECAY = -80.0
ROW_GROUP = 64
NORM_ROWS = 32
LN_ROWS = 16
FFN_DOWN_GROUP = 4

F32 = jnp.float32
BF16 = jnp.bfloat16


def _dot(a, b):
    return jnp.dot(a, b, preferred_element_type=F32)


def _dot_nt(a, b):
    return lax.dot_general(a, b, (((1,), (1,)), ((), ())), preferred_element_type=F32)


def _dot_tn(a, b):
    return lax.dot_general(a, b, (((0,), (0,)), ((), ())), preferred_element_type=F32)


def _rmsnorm_rows(x, w):
    ms = jnp.mean(x * x, axis=-1, keepdims=True)
    return (x * lax.rsqrt(ms + EPS)) * w


def _rmsnorm_to(x_ref, w_ref, h_ref, rows):
    step = min(NORM_ROWS, rows)
    for g in range(rows // step):
        rs = pl.ds(g * step, step)
        h_ref[rs, :] = _rmsnorm_rows(x_ref[rs, :], w_ref[...]).astype(BF16)


def _conv_mixer_kernel(x_ref, hist_ref, nw_ref, w1_ref, b1_ref, wdw_ref, bdw_ref, lng_ref, lnb_ref,
                       w2_ref, b2_ref, xo_ref, histo_ref, h_s, p_s, y_s, z_s, *, rows):
    D = D_MODEL
    nlb = D // LANES
    first = CONV_PAD - CONV_HIST

    @pl.when(pl.program_id(1) == 0)
    def _():
        for lb in range(nlb):
            p_s[lb, pl.ds(first, CONV_HIST), :] = hist_ref[:, lb * LANES:(lb + 1) * LANES]

    _rmsnorm_to(x_ref, nw_ref, h_s, rows)

    h = h_s[...]
    per = MXU_N // LANES
    rg = min(ROW_GROUP, rows)
    ncb = D // MXU_N

    def pointwise_glu(cb):
        ca = slice(cb * MXU_N, (cb + 1) * MXU_N)
        cg = slice(D + cb * MXU_N, D + (cb + 1) * MXU_N)
        ua = _dot(h, w1_ref[:, ca]) + b1_ref[:, ca]
        ug = _dot(h, w1_ref[:, cg]) + b1_ref[:, cg]
        glu = ua * jax.nn.sigmoid(ug)
        for j in range(per):
            p_s[cb * per + j, pl.ds(CONV_PAD, rows), :] = glu[:, j * LANES:(j + 1) * LANES]

    def depthwise(lb):
        taps = [jnp.broadcast_to(wdw_ref[lb, k:k + 1, :], (rg, LANES)) for k in range(CONV_K)]
        bias = jnp.broadcast_to(bdw_ref[lb], (rg, LANES))
        for g in range(rows // rg):
            acc = bias
            for k in range(CONV_K):
                acc = acc + taps[k] * p_s[lb, pl.ds(g * rg + first + k, rg), :]
            y_s[lb, pl.ds(g * rg, rg), :] = acc

    pointwise_glu(0)
    for cb in range(ncb):
        if cb + 1 < ncb:
            pointwise_glu(cb + 1)
        for j in range(per):
            depthwise(cb * per + j)

    nr = min(LN_ROWS, rows)
    for g in range(rows // nr):
        rs = pl.ds(g * nr, nr)
        y = jnp.concatenate([y_s[lb, rs, :] for lb in range(nlb)], axis=1)
        mu = jnp.mean(y, axis=-1, keepdims=True)
        d = y - mu
        var = jnp.mean(d * d, axis=-1, keepdims=True)
        yn = (d * lax.rsqrt(var + EPS)) * lng_ref[...] + lnb_ref[...]
        z_s[rs, :] = (yn * jax.nn.sigmoid(yn)).astype(BF16)

    for lb in range(nlb):
        tail = p_s[lb, pl.ds(rows + first, CONV_HIST), :]
        p_s[lb, pl.ds(first, CONV_HIST), :] = tail
        histo_ref[:, lb * LANES:(lb + 1) * LANES] = tail

    out = _dot(z_s[...], w2_ref[...]) + b2_ref[...]
    xo_ref[...] = x_ref[...] + out


def _const_spec(shape):
    zeros = (0,) * len(shape)
    return pl.BlockSpec(shape, lambda b, s: zeros, pipeline_mode=pl.Buffered(1))


def _params():
    return pltpu.CompilerParams(dimension_semantics=("arbitrary", "arbitrary"),
                                vmem_limit_bytes=VMEM_LIMIT_BYTES)


def _row2(v):
    return v.reshape(1, -1)


def _lane_blocks(v):
    r, d = v.shape
    return v.reshape(r, d // LANES, LANES).transpose(1, 0, 2)


def _conv_mixer(x, hist, nw, w1, b1, wdw, bdw, lng, lnb, w2, b2, *, rows):
    B, S, D = x.shape
    grid = (B, S // rows)
    x_spec = pl.BlockSpec((None, rows, D), lambda b, s: (b, s, 0))
    hist_spec = pl.BlockSpec((None, CONV_HIST, D), lambda b, s: (b, 0, 0))
    return pl.pallas_call(
        functools.partial(_conv_mixer_kernel, rows=rows),
        grid=grid,
        in_specs=[x_spec, hist_spec, _const_spec((1, D)), _const_spec((D, 2 * D)),
                  _const_spec((1, 2 * D)), _const_spec((D // LANES, CONV_K, LANES)),
                  _const_spec((D // LANES, 1, LANES)), _const_spec((1, D)), _const_spec((1, D)),
                  _const_spec((D, D)), _const_spec((1, D))],
        out_specs=[x_spec, hist_spec],
        out_shape=[jax.ShapeDtypeStruct(x.shape, x.dtype),
                   jax.ShapeDtypeStruct(hist.shape, hist.dtype)],
        scratch_shapes=[pltpu.VMEM((rows, D), BF16),
                        pltpu.VMEM((D // LANES, CONV_PAD + rows, LANES), F32),
                        pltpu.VMEM((D // LANES, rows, LANES), F32),
                        pltpu.VMEM((rows, D), BF16)],
        compiler_params=_params(),
        name="conv_mixer",
    )(x, hist, _row2(nw), w1.astype(BF16), _row2(b1), _lane_blocks(wdw), _lane_blocks(_row2(bdw)),
      _row2(lng), _row2(lnb), w2.astype(BF16), _row2(b2))


def _hgrn_mixer_kernel(x_ref, s0_ref, nw_ref, win_ref, lbl_ref, hnw_ref, wo_ref, xo_ref, so_ref,
                       h_s, q_s, k_s, i_s, lf_s, bc_s, g_s, o_s, z_s, st_s, sb_s, *, rows, chunk, layer):
    D = D_MODEL
    nh = HG_HEADS
    dh = HG_DH

    @pl.when(pl.program_id(1) == 0)
    def _():
        for hd in range(nh):
            st_s[hd] = s0_ref[hd].T
            sb_s[hd] = s0_ref[hd].astype(BF16)

    _rmsnorm_to(x_ref, nw_ref, h_s, rows)

    logits = lbl_ref[...]
    m = jnp.max(logits, axis=0, keepdims=True)
    e = jnp.exp(logits - m)
    p = e / jnp.sum(e, axis=0, keepdims=True)
    csum = p[0:1, :]
    for j in range(1, layer + 1):
        csum = csum + p[j:j + 1, :]
    lb = csum - p[0:1, :]

    h = h_s[...]
    for cb in range(D // MXU_N):
        cols = slice(cb * MXU_N, (cb + 1) * MXU_N)
        q_s[:, cols] = _dot(h, win_ref[:, cb * MXU_N:(cb + 1) * MXU_N])
        fl = _dot(h, win_ref[:, D + cb * MXU_N:D + (cb + 1) * MXU_N])
        f = lb[:, cols] + (1.0 - lb[:, cols]) * jax.nn.sigmoid(fl)
        lf_s[:, cols] = jnp.log(f)
        k_s[:, cols] = 1.0 - f
        i_s[:, cols] = _dot(h, win_ref[:, 2 * D + cb * MXU_N:2 * D + (cb + 1) * MXU_N]).astype(BF16)
        gt = _dot(h, win_ref[:, 3 * D + cb * MXU_N:3 * D + (cb + 1) * MXU_N])
        g_s[:, cols] = hnw_ref[:, cols] * (gt * jax.nn.sigmoid(gt))

    ri =lax.broadcasted_iota(jnp.int32, (chunk, chunk), 0)
    ci = lax.broadcasted_iota(jnp.int32, (chunk, chunk), 1)
    causal = ri >= ci
    tri = causal.astype(BF16)

    mn = jnp.zeros((1, D), F32)
    for c in range(rows // chunk):
        rs = pl.ds(c * chunk, chunk)
        lf = lf_s[rs, :]
        lf_hi = lf.astype(BF16)
        lf_lo = (lf - lf_hi.astype(F32)).astype(BF16)
        bc = _dot(tri, lf_hi) + _dot(tri, lf_lo)
        bc_s[rs, :] = bc
        mn = jnp.minimum(mn, bc[chunk - 1:chunk, :])
    safe = jnp.min(mn) >= MIN_CHUNK_LOG_DECAY

    def chunk_step(c):
        rs = pl.ds(c * chunk, chunk)
        bc = bc_s[rs, :]
        b_last = bc[chunk - 1:chunk, :]
        qd = (q_s[rs, :] * jnp.exp(bc)).astype(BF16)
        kk = k_s[rs, :]
        kd = (kk * jnp.exp(-bc)).astype(BF16)
        ke = (kk * jnp.exp(b_last - bc)).astype(BF16)
        dec = jnp.exp(b_last)
        iv = i_s[rs, :]
        for hd in range(nh):
            cols = slice(hd * dh, (hd + 1) * dh)
            sc = jnp.where(causal, _dot_nt(qd[:, cols], kd[:, cols]), 0.0)
            o_s[rs, cols] = _dot(sc.astype(BF16), iv[:, cols]) + _dot(qd[:, cols], sb_s[hd])
            st = st_s[hd] * dec[:, cols] + _dot_tn(iv[:, cols], ke[:, cols])
            st_s[hd] = st
            sb_s[hd] = st.astype(BF16).T

    row_id = lax.broadcasted_iota(jnp.int32, (chunk, 1), 0)

    def slow_chunk_body(c, carry):
        r = pl.multiple_of(c * chunk, chunk)
        lf = lf_s[pl.ds(r, chunk), :]
        qc = q_s[pl.ds(r, chunk), :].astype(BF16)
        kc = k_s[pl.ds(r, chunk), :].astype(BF16)
        iv = i_s[pl.ds(r, chunk), :]
        o_s[pl.ds(r, chunk), :] = jnp.zeros((chunk, D), F32)

        def token_body(t, carry2):
            sel = row_id == t
            f_row = jnp.exp(jnp.sum(jnp.where(sel, lf, 0.0), axis=0, keepdims=True))
            i_sel = jnp.where(sel, iv, jnp.zeros_like(iv))
            for hd in range(nh):
                cols = slice(hd * dh, (hd + 1) * dh)
                st = st_s[hd] * f_row[:, cols] + _dot_tn(i_sel[:, cols], kc[:, cols])
                st_s[hd] = st
                res = _dot_nt(qc[:, cols], st.astype(BF16))
                o_s[pl.ds(r, chunk), cols] = jnp.where(sel, res, o_s[pl.ds(r, chunk), cols])
            return carry2

        lax.fori_loop(0, chunk, token_body, 0)
        return carry

    @pl.when(safe)
    def _():
        for c in range(rows // chunk):
            chunk_step(c)

    @pl.when(jnp.logical_not(safe))
    def _():
        lax.fori_loop(0, rows // chunk, slow_chunk_body, 0)
        for hd in range(nh):
            sb_s[hd] = st_s[hd].astype(BF16).T

    nr = min(NORM_ROWS, rows)
    for g in range(rows // nr):
        rs = pl.ds(g * nr, nr)
        for hd in range(nh):
            cols = slice(hd * dh, (hd + 1) * dh)
            o = o_s[rs, cols]
            ms = jnp.mean(o * o, axis=-1, keepdims=True)
            on = (o * lax.rsqrt(ms + EPS)) * g_s[rs, cols]
            z_s[rs, cols] = on.astype(BF16)

    xo_ref[...] = x_ref[...] + _dot(z_s[...], wo_ref[...])

    @pl.when(pl.program_id(1) == pl.num_programs(1) - 1)
    def _():
        for hd in range(nh):
            so_ref[hd] = st_s[hd].T


def _hgrn_mixer(x, s0, nw, win, lb_logits, hnw, wo, *, rows, layer):
    B, S, D = x.shape
    chunk = min(HG_CHUNK, rows)
    grid = (B, S // rows)
    x_spec = pl.BlockSpec((None, rows, D), lambda b, s: (b, s, 0))
    st_spec = pl.BlockSpec((None, HG_HEADS, HG_DH, HG_DH), lambda b, s: (b, 0, 0, 0))
    n_layers = lb_logits.shape[0]
    return pl.pallas_call(
        functools.partial(_hgrn_mixer_kernel, rows=rows, chunk=chunk, layer=layer),
        grid=grid,
        in_specs=[x_spec, st_spec, _const_spec((1, D)), _const_spec((D, 4 * D)),
                  _const_spec((n_layers, D)), _const_spec((1, D)), _const_spec((D, D))],
        out_specs=[x_spec, st_spec],
        out_shape=[jax.ShapeDtypeStruct(x.shape, x.dtype),
                   jax.ShapeDtypeStruct(s0.shape, s0.dtype)],
        scratch_shapes=[pltpu.VMEM((rows, D), BF16),
                        pltpu.VMEM((rows, D), F32),
                        pltpu.VMEM((rows, D), F32),
                        pltpu.VMEM((rows, D), BF16),
                        pltpu.VMEM((rows, D), F32),
                        pltpu.VMEM((rows, D), F32),
                        pltpu.VMEM((rows, D), F32),
                        pltpu.VMEM((rows, D), F32),
                        pltpu.VMEM((rows, D), BF16),
                        pltpu.VMEM((HG_HEADS, HG_DH, HG_DH), F32),
                        pltpu.VMEM((HG_HEADS, HG_DH, HG_DH), BF16)],
        compiler_params=_params(),
        name="hgrn_mixer",
    )(x, s0, _row2(nw), win.astype(BF16), lb_logits, _row2(hnw), wo.astype(BF16))


def _ffn_kernel(x_ref, buf_ref, nw_ref, wup_ref, bup_ref, wdw_ref, bdw_ref, wdn_ref, bdn_ref, fw_ref,
                xo_ref, bufo_ref, h_s, u_s, acc_s, *, rows, final_norm):
    F = D_FF
    first = FFN_PAD - FFN_HIST
    nlb = 2 * F // LANES
    per = MXU_N // LANES
    nfb = F // MXU_N

    @pl.when(pl.program_id(1) == 0)
    def _():
        for lb in range(nlb):
            u_s[lb, pl.ds(first, FFN_HIST), :] = buf_ref[:, lb * LANES:(lb + 1) * LANES]

    _rmsnorm_to(x_ref, nw_ref, h_s, rows)

    def conv_piece(lb):
        cols = slice(lb * LANES, (lb + 1) * LANES)
        c = bdw_ref[:, cols]
        for k in range(FFN_K):
            c = c + wdw_ref[k:k + 1, cols] * u_s[lb, pl.ds(first + k, rows), :]
        return c

    h = h_s[...]

    def up_proj(b):
        for part in range(2):
            cols = slice(part * F + b * MXU_N, part * F + (b + 1) * MXU_N)
            u = _dot(h, wup_ref[:, cols]) + bup_ref[:, cols]
            for j in range(per):
                lb = (part * F + b * MXU_N) // LANES + j
                u_s[lb, pl.ds(FFN_PAD, rows), :] = u[:, j * LANES:(j + 1) * LANES]

    def act_block(b):
        out = []
        for j in range(per):
            a = conv_piece(b * per + j)
            v = conv_piece(F // LANES + b * per + j)
            out.append(((a * jax.nn.sigmoid(a)) * v).astype(BF16))
        return out

    groups = [(fb, min(FFN_DOWN_GROUP, nfb - fb)) for fb in range(0, nfb, FFN_DOWN_GROUP)]

    for b in range(groups[0][1]):
        up_proj(b)
    for gi, (fb, group) in enumerate(groups):
        if gi + 1 < len(groups):
            nxt, n_nxt = groups[gi + 1]
            for b in range(nxt, nxt + n_nxt):
                up_proj(b)
        acts = []
        for b in range(fb, fb + group):
            acts += act_block(b)
        contrib = _dot(jnp.concatenate(acts, axis=1), wdn_ref[fb * MXU_N:(fb + group) * MXU_N, :])
        if gi == 0:
            acc_s[...] = contrib
        else:
            acc_s[...] += contrib

    for lb in range(nlb):
        tail = u_s[lb, pl.ds(rows + first, FFN_HIST), :]
        u_s[lb, pl.ds(first, FFN_HIST), :] = tail
        bufo_ref[:, lb * LANES:(lb + 1) * LANES] = tail

    step = min(NORM_ROWS, rows)
    for g in range(rows // step):
        rs = pl.ds(g * step, step)
        y = x_ref[rs, :] + (acc_s[rs, :] + bdn_ref[...])
        if final_norm:
            y = _rmsnorm_rows(y, fw_ref[...])
        xo_ref[rs, :] = y


def _conv_ffn(x, buf, nw, wup, bup, wdw, bdw, wdn, bdn, fw, *, rows, final_norm):
    B, S, D = x.shape
    F = D_FF
    grid = (B, S // rows)
    x_spec = pl.BlockSpec((None, rows, D), lambda b, s: (b, s, 0))
    buf_spec = pl.BlockSpec((None, FFN_HIST, 2 * F), lambda b, s: (b, 0, 0))
    return pl.pallas_call(
        functools.partial(_ffn_kernel, rows=rows, final_norm=final_norm),
        grid=grid,
        in_specs=[x_spec, buf_spec, _const_spec((1, D)), _const_spec((D, 2 * F)),
                  _const_spec((1, 2 * F)), _const_spec((FFN_K, 2 * F)), _const_spec((1, 2 * F)),
                  _const_spec((F, D)), _const_spec((1, D)), _const_spec((1, D))],
        out_specs=[x_spec, buf_spec],
        out_shape=[jax.ShapeDtypeStruct(x.shape, x.dtype),
                   jax.ShapeDtypeStruct(buf.shape, buf.dtype)],
        scratch_shapes=[pltpu.VMEM((rows, D), BF16),
                        pltpu.VMEM((2 * F // LANES, FFN_PAD + rows, LANES), F32),
                        pltpu.VMEM((rows, D), F32)],
        compiler_params=_params(),
        name="conv_ffn",
    )(x, buf, _row2(nw), wup.astype(BF16), _row2(bup), wdw, _row2(bdw), wdn.astype(BF16),
      _row2(bdn), _row2(fw))


def _trunk(x, conv_bufs, hgrn_states, ffn_bufs, P, rows):
    depth = P['norm_mix_w'].shape[0]
    conv_new, hgrn_new, ffn_new = [], [], []
    for i in range(depth):
        j = i // 2
        if i % 2 == 0:
            x, nb = _conv_mixer(x, conv_bufs[j], P['norm_mix_w'][i], P['cv_w_pw1'][j],
                                P['cv_b_pw1'][j], P['cv_w_dw'][j], P['cv_b_dw'][j], P['cv_ln_g'][j],
                                P['cv_ln_b'][j], P['cv_w_pw2'][j], P['cv_b_pw2'][j], rows=rows)
            conv_new.append(nb)
        else:
            x, ns = _hgrn_mixer(x, hgrn_states[j], P['norm_mix_w'][i], P['hg_w_in'][j],
                                P['hg_lb_logits'], P['hg_norm_w'][j], P['hg_w_o'][j],
                                rows=rows, layer=j)
            hgrn_new.append(ns)
        x, fb = _conv_ffn(x, ffn_bufs[i], P['norm_ffn_w'][i], P['ff_w_up'][i], P['ff_b_up'][i],
                          P['ff_w_dw'][i], P['ff_b_dw'][i], P['ff_w_down'][i], P['ff_b_down'][i],
                          P['norm_final_w'], rows=rows, final_norm=(i == depth - 1))
        ffn_new.append(fb)
    return x, jnp.stack(conv_new), jnp.stack(hgrn_new), jnp.stack(ffn_new)


def kernel(x_prompt, x_sample, cache_conv, state_hgrn, cache_ffn, norm_mix_w, norm_ffn_w, norm_final_w, cv_w_pw1, cv_b_pw1, cv_w_dw, cv_b_dw, cv_ln_g, cv_ln_b, cv_w_pw2, cv_b_pw2, hg_w_in, hg_lb_logits, hg_norm_w, hg_w_o, ff_w_up, ff_b_up, ff_w_dw, ff_b_dw, ff_w_down, ff_b_down):
    P = dict(norm_mix_w=norm_mix_w, norm_ffn_w=norm_ffn_w, norm_final_w=norm_final_w,
             cv_w_pw1=cv_w_pw1, cv_b_pw1=cv_b_pw1, cv_w_dw=cv_w_dw, cv_b_dw=cv_b_dw,
             cv_ln_g=cv_ln_g, cv_ln_b=cv_ln_b, cv_w_pw2=cv_w_pw2, cv_b_pw2=cv_b_pw2,
             hg_w_in=hg_w_in, hg_lb_logits=hg_lb_logits, hg_norm_w=hg_norm_w, hg_w_o=hg_w_o,
             ff_w_up=ff_w_up, ff_b_up=ff_b_up, ff_w_dw=ff_w_dw, ff_b_dw=ff_b_dw,
             ff_w_down=ff_w_down, ff_b_down=ff_b_down)
    bp, sp, _ = x_prompt.shape
    n_conv = cache_conv.shape[0]
    n_hgrn = state_hgrn.shape[0]
    depth = cache_ffn.shape[0]
    zero_conv = jnp.zeros((n_conv, bp, CONV_HIST, D_MODEL), x_prompt.dtype)
    zero_hgrn = jnp.zeros((n_hgrn, bp, HG_HEADS, HG_DH, HG_DH), F32)
    zero_ffn = jnp.zeros((depth, bp, FFN_HIST, 2 * D_FF), x_prompt.dtype)
    y_p, conv_p, hgrn_p, ffn_p = _trunk(x_prompt, zero_conv, zero_hgrn, zero_ffn, P,
                                        min(PROMPT_TILE, sp))
    y_s, conv_s, hgrn_s, ffn_s = _trunk(x_sample, cache_conv, state_hgrn, cache_ffn, P,
                                        x_sample.shape[1])
    return (y_p, y_s, conv_p, hgrn_p, ffn_p, conv_s, hgrn_s, ffn_s)
```

```python
import functools

import jax
import jax.numpy as jnp
from jax import lax
from jax.experimental import pallas as pl
from jax.experimental.pallas import tpu as pltpu

D_MODEL = 1024
CONV_K = 31
HG_HEADS = 8
HG_DH = D_MODEL // HG_HEADS
D_FF = 2816
FFN_K = 3
EPS = 1e-6

LANES = 128
SUBLANES = 8
MXU_N = 256
VMEM_LIMIT_BYTES = 56 * 1024 * 1024

CONV_HIST = CONV_K - 1
CONV_PAD = 32
FFN_HIST = FFN_K - 1
FFN_PAD = SUBLANES
PROMPT_TILE = 512
HG_CHUNK = 64
MIN_HALF_CHUNK_LOG_DECAY = -60.0
ROW_GROUP = 64
NORM_ROWS = 32
LN_ROWS = 16
FFN_DOWN_GROUP = 6

F32 = jnp.float32
BF16 = jnp.bfloat16


def _dot(a, b):
    return jnp.dot(a, b, preferred_element_type=F32)


def _dot_nt(a, b):
    return lax.dot_general(a, b, (((1,), (1,)), ((), ())), preferred_element_type=F32)


def _dot_tn(a, b):
    return lax.dot_general(a, b, (((0,), (0,)), ((), ())), preferred_element_type=F32)


def _rmsnorm_rows(x, w):
    ms = jnp.mean(x * x, axis=-1, keepdims=True)
    return (x * lax.rsqrt(ms + EPS)) * w


def _rmsnorm_to(x_ref, w_ref, h_ref, rows):
    step = min(NORM_ROWS, rows)
    for g in range(rows // step):
        rs = pl.ds(g * step, step)
        h_ref[rs, :] = _rmsnorm_rows(x_ref[rs, :], w_ref[...]).astype(BF16)


def _conv_mixer_kernel(x_ref, hist_ref, nw_ref, w1_ref, b1_ref, wdw_ref, bdw_ref, lng_ref, lnb_ref,
                       w2_ref, b2_ref, xo_ref, histo_ref, h_s, p_s, y_s, z_s, *, rows):
    D = D_MODEL
    nlb = D // LANES
    first = CONV_PAD - CONV_HIST

    @pl.when(pl.program_id(1) == 0)
    def _():
        for lb in range(nlb):
            p_s[lb, pl.ds(first, CONV_HIST), :] = hist_ref[:, lb * LANES:(lb + 1) * LANES]

    _rmsnorm_to(x_ref, nw_ref, h_s, rows)

    h = h_s[...]
    per = MXU_N // LANES
    rg = min(ROW_GROUP, rows)
    ncb = D // MXU_N

    def pointwise_glu(cb):
        ca = slice(cb * MXU_N, (cb + 1) * MXU_N)
        cg = slice(D + cb * MXU_N, D + (cb + 1) * MXU_N)
        ua = _dot(h, w1_ref[:, ca]) + b1_ref[:, ca]
        ug = _dot(h, w1_ref[:, cg]) + b1_ref[:, cg]
        glu = ua * jax.nn.sigmoid(ug)
        for j in range(per):
            p_s[cb * per + j, pl.ds(CONV_PAD, rows), :] = glu[:, j * LANES:(j + 1) * LANES]

    def depthwise(lb):
        taps = [jnp.broadcast_to(wdw_ref[lb, k:k + 1, :], (rg, LANES)) for k in range(CONV_K)]
        bias = jnp.broadcast_to(bdw_ref[lb], (rg, LANES))
        for g in range(rows // rg):
            acc = bias
            for k in range(CONV_K):
                acc = acc + taps[k] * p_s[lb, pl.ds(g * rg + first + k, rg), :]
            y_s[lb, pl.ds(g * rg, rg), :] = acc

    pointwise_glu(0)
    for cb in range(ncb):
        if cb + 1 < ncb:
            pointwise_glu(cb + 1)
        for j in range(per):
            depthwise(cb * per + j)

    nr = min(LN_ROWS, rows)
    for g in range(rows // nr):
        rs = pl.ds(g * nr, nr)
        y = jnp.concatenate([y_s[lb, rs, :] for lb in range(nlb)], axis=1)
        mu = jnp.mean(y, axis=-1, keepdims=True)
        d = y - mu
        var = jnp.mean(d * d, axis=-1, keepdims=True)
        yn = (d * lax.rsqrt(var + EPS)) * lng_ref[...] + lnb_ref[...]
        z_s[rs, :] = (yn * jax.nn.sigmoid(yn)).astype(BF16)

    for lb in range(nlb):
        tail = p_s[lb, pl.ds(rows + first, CONV_HIST), :]
        p_s[lb, pl.ds(first, CONV_HIST), :] = tail
        histo_ref[:, lb * LANES:(lb + 1) * LANES] = tail

    out = _dot(z_s[...], w2_ref[...]) + b2_ref[...]
    xo_ref[...] = x_ref[...] + out


def _const_spec(shape):
    zeros = (0,) * len(shape)
    return pl.BlockSpec(shape, lambda b, s: zeros, pipeline_mode=pl.Buffered(1))


def _params():
    return pltpu.CompilerParams(dimension_semantics=("arbitrary", "arbitrary"),
                                vmem_limit_bytes=VMEM_LIMIT_BYTES)


def _row2(v):
    return v.reshape(1, -1)


def _lane_blocks(v):
    r, d = v.shape
    return v.reshape(r, d // LANES, LANES).transpose(1, 0, 2)


def _conv_mixer(x, hist, nw, w1, b1, wdw, bdw, lng, lnb, w2, b2, *, rows):
    B, S, D = x.shape
    grid = (B, S // rows)
    x_spec = pl.BlockSpec((None, rows, D), lambda b, s: (b, s, 0))
    hist_spec = pl.BlockSpec((None, CONV_HIST, D), lambda b, s: (b, 0, 0))
    return pl.pallas_call(
        functools.partial(_conv_mixer_kernel, rows=rows),
        grid=grid,
        in_specs=[x_spec, hist_spec, _const_spec((1, D)), _const_spec((D, 2 * D)),
                  _const_spec((1, 2 * D)), _const_spec((D // LANES, CONV_K, LANES)),
                  _const_spec((D // LANES, 1, LANES)), _const_spec((1, D)), _const_spec((1, D)),
                  _const_spec((D, D)), _const_spec((1, D))],
        out_specs=[x_spec, hist_spec],
        out_shape=[jax.ShapeDtypeStruct(x.shape, x.dtype),
                   jax.ShapeDtypeStruct(hist.shape, hist.dtype)],
        scratch_shapes=[pltpu.VMEM((rows, D), BF16),
                        pltpu.VMEM((D // LANES, CONV_PAD + rows, LANES), F32),
                        pltpu.VMEM((D // LANES, rows, LANES), F32),
                        pltpu.VMEM((rows, D), BF16)],
        compiler_params=_params(),
        name="conv_mixer",
    )(x, hist, _row2(nw), w1.astype(BF16), _row2(b1), _lane_blocks(wdw), _lane_blocks(_row2(bdw)),
      _row2(lng), _row2(lnb), w2.astype(BF16), _row2(b2))


def _hgrn_mixer_kernel(x_ref, s0_ref, nw_ref, win_ref, lbl_ref, hnw_ref, wo_ref, xo_ref, so_ref,
                       h_s, q_s, k_s, i_s, lf_s, bc_s, g_s, o_s, z_s, st_s, sb_s, *, rows, chunk, layer):
    D = D_MODEL
    nh = HG_HEADS
    dh = HG_DH

    @pl.when(pl.program_id(1) == 0)
    def _():
        for hd in range(nh):
            st_s[hd] = s0_ref[hd].T
            sb_s[hd] = s0_ref[hd].astype(BF16)

    _rmsnorm_to(x_ref, nw_ref, h_s, rows)

    logits = lbl_ref[...]
    m = jnp.max(logits, axis=0, keepdims=True)
    e = jnp.exp(logits - m)
    p = e / jnp.sum(e, axis=0, keepdims=True)
    csum = p[0:1, :]
    for j in range(1, layer + 1):
        csum = csum + p[j:j + 1, :]
    lb = csum - p[0:1, :]

    h = h_s[...]
    for cb in range(D // MXU_N):
        cols = slice(cb * MXU_N, (cb + 1) * MXU_N)
        q_s[:, cols] = _dot(h, win_ref[:, cb * MXU_N:(cb + 1) * MXU_N])
        fl = _dot(h, win_ref[:, D + cb * MXU_N:D + (cb + 1) * MXU_N])
        f = lb[:, cols] + (1.0 - lb[:, cols]) * jax.nn.sigmoid(fl)
        lf_s[:, cols] = jnp.log(f)
        k_s[:, cols] = 1.0 - f
        i_s[:, cols] = _dot(h, win_ref[:, 2 * D + cb * MXU_N:2 * D + (cb + 1) * MXU_N]).astype(BF16)
        gt = _dot(h, win_ref[:, 3 * D + cb * MXU_N:3 * D + (cb + 1) * MXU_N])
        g_s[:, cols] = hnw_ref[:, cols] * (gt * jax.nn.sigmoid(gt))

    ri =lax.broadcasted_iota(jnp.int32, (chunk, chunk), 0)
    ci = lax.broadcasted_iota(jnp.int32, (chunk, chunk), 1)
    causal = ri >= ci
    tri = causal.astype(BF16)

    mid = chunk // 2
    mn = jnp.zeros((1, D), F32)
    for c in range(rows // chunk):
        rs = pl.ds(c * chunk, chunk)
        lf = lf_s[rs, :]
        lf_hi = lf.astype(BF16)
        lf_lo = (lf - lf_hi.astype(F32)).astype(BF16)
        bc = _dot(tri, lf_hi) + _dot(tri, lf_lo)
        bc_s[rs, :] = bc
        b_mid = bc[mid - 1:mid, :]
        mn = jnp.minimum(mn, jnp.minimum(b_mid, bc[chunk - 1:chunk, :] - b_mid))
    safe = jnp.min(mn) >= MIN_HALF_CHUNK_LOG_DECAY

    def chunk_step(c):
        rs = pl.ds(c * chunk, chunk)
        for hd in range(nh):
            cols = slice(hd * dh, (hd + 1) * dh)
            bc = bc_s[rs, cols]
            b_mid = bc[mid - 1:mid, :]
            b_last = bc[chunk - 1:chunk, :]
            q = q_s[rs, cols]
            kk = k_s[rs, cols]
            qi = (q * jnp.exp(bc)).astype(BF16)
            qd = (q * jnp.exp(bc - b_mid)).astype(BF16)
            kd = (kk * jnp.exp(b_mid - bc)).astype(BF16)
            ke = (kk * jnp.exp(b_last - bc)).astype(BF16)
            iv = i_s[rs, cols]
            sc = jnp.where(causal, _dot_nt(qd, kd), 0.0)
            o_s[rs, cols] = _dot(sc.astype(BF16), iv) + _dot(qi, sb_s[hd])
            st = st_s[hd] * jnp.exp(b_last) + _dot_tn(iv, ke)
            st_s[hd] = st
            sb_s[hd] = st.astype(BF16).T

    row_id = lax.broadcasted_iota(jnp.int32, (chunk, 1), 0)

    def slow_chunk_body(c, carry):
        r = pl.multiple_of(c * chunk, chunk)
        lf = lf_s[pl.ds(r, chunk), :]
        qc = q_s[pl.ds(r, chunk), :].astype(BF16)
        kc = k_s[pl.ds(r, chunk), :].astype(BF16)
        iv = i_s[pl.ds(r, chunk), :]
        o_s[pl.ds(r, chunk), :] = jnp.zeros((chunk, D), F32)

        def token_body(t, carry2):
            sel = row_id == t
            f_row = jnp.exp(jnp.sum(jnp.where(sel, lf, 0.0), axis=0, keepdims=True))
            i_sel = jnp.where(sel, iv, jnp.zeros_like(iv))
            for hd in range(nh):
                cols = slice(hd * dh, (hd + 1) * dh)
                st = st_s[hd] * f_row[:, cols] + _dot_tn(i_sel[:, cols], kc[:, cols])
                st_s[hd] = st
                res = _dot_nt(qc[:, cols], st.astype(BF16))
                o_s[pl.ds(r, chunk), cols] = jnp.where(sel, res, o_s[pl.ds(r, chunk), cols])
            return carry2

        lax.fori_loop(0, chunk, token_body, 0)
        return carry

    @pl.when(safe)
    def _():
        for c in range(rows // chunk):
            chunk_step(c)

    @pl.when(jnp.logical_not(safe))
    def _():
        lax.fori_loop(0, rows // chunk, slow_chunk_body, 0)
        for hd in range(nh):
            sb_s[hd] = st_s[hd].astype(BF16).T

    nr = min(NORM_ROWS, rows)
    for g in range(rows // nr):
        rs = pl.ds(g * nr, nr)
        for hd in range(nh):
            cols = slice(hd * dh, (hd + 1) * dh)
            o = o_s[rs, cols]
            ms = jnp.mean(o * o, axis=-1, keepdims=True)
            on = (o * lax.rsqrt(ms + EPS)) * g_s[rs, cols]
            z_s[rs, cols] = on.astype(BF16)

    xo_ref[...] = x_ref[...] + _dot(z_s[...], wo_ref[...])

    @pl.when(pl.program_id(1) == pl.num_programs(1) - 1)
    def _():
        for hd in range(nh):
            so_ref[hd] = st_s[hd].T


def _hgrn_mixer(x, s0, nw, win, lb_logits, hnw, wo, *, rows, layer):
    B, S, D = x.shape
    chunk = min(HG_CHUNK, rows)
    grid = (B, S // rows)
    x_spec = pl.BlockSpec((None, rows, D), lambda b, s: (b, s, 0))
    st_spec = pl.BlockSpec((None, HG_HEADS, HG_DH, HG_DH), lambda b, s: (b, 0, 0, 0))
    n_layers = lb_logits.shape[0]
    return pl.pallas_call(
        functools.partial(_hgrn_mixer_kernel, rows=rows, chunk=chunk, layer=layer),
        grid=grid,
        in_specs=[x_spec, st_spec, _const_spec((1, D)), _const_spec((D, 4 * D)),
                  _const_spec((n_layers, D)), _const_spec((1, D)), _const_spec((D, D))],
        out_specs=[x_spec, st_spec],
        out_shape=[jax.ShapeDtypeStruct(x.shape, x.dtype),
                   jax.ShapeDtypeStruct(s0.shape, s0.dtype)],
        scratch_shapes=[pltpu.VMEM((rows, D), BF16),
                        pltpu.VMEM((rows, D), F32),
                        pltpu.VMEM((rows, D), F32),
                        pltpu.VMEM((rows, D), BF16),
                        pltpu.VMEM((rows, D), F32),
                        pltpu.VMEM((rows, D), F32),
                        pltpu.VMEM((rows, D), F32),
                        pltpu.VMEM((rows, D), F32),
                        pltpu.VMEM((rows, D), BF16),
                        pltpu.VMEM((HG_HEADS, HG_DH, HG_DH), F32),
                        pltpu.VMEM((HG_HEADS, HG_DH, HG_DH), BF16)],
        compiler_params=_params(),
        name="hgrn_mixer",
    )(x, s0, _row2(nw), win.astype(BF16), lb_logits, _row2(hnw), wo.astype(BF16))


def _ffn_kernel(x_ref, buf_ref, nw_ref, wup_ref, bup_ref, wdw_ref, bdw_ref, wdn_ref, bdn_ref, fw_ref,
                xo_ref, bufo_ref, h_s, u_s, acc_s, *, rows, final_norm):
    F = D_FF
    first = FFN_PAD - FFN_HIST
    nlb = 2 * F // LANES
    per = MXU_N // LANES
    nfb = F // MXU_N

    @pl.when(pl.program_id(1) == 0)
    def _():
        for lb in range(nlb):
            u_s[lb, pl.ds(first, FFN_HIST), :] = buf_ref[:, lb * LANES:(lb + 1) * LANES]

    _rmsnorm_to(x_ref, nw_ref, h_s, rows)

    def conv_piece(lb):
        cols = slice(lb * LANES, (lb + 1) * LANES)
        c = bdw_ref[:, cols]
        for k in range(FFN_K):
            c = c + wdw_ref[k:k + 1, cols] * u_s[lb, pl.ds(first + k, rows), :]
        return c

    h = h_s[...]

    def up_proj(b):
        for part in range(2):
            cols = slice(part * F + b * MXU_N, part * F + (b + 1) * MXU_N)
            u = _dot(h, wup_ref[:, cols]) + bup_ref[:, cols]
            for j in range(per):
                lb = (part * F + b * MXU_N) // LANES + j
                u_s[lb, pl.ds(FFN_PAD, rows), :] = u[:, j * LANES:(j + 1) * LANES]

    def act_block(b):
        out = []
        for j in range(per):
            a = conv_piece(b * per + j)
            v = conv_piece(F // LANES + b * per + j)
            out.append(((a * jax.nn.sigmoid(a)) * v).astype(BF16))
        return out

    groups = [(fb, min(FFN_DOWN_GROUP, nfb - fb)) for fb in range(0, nfb, FFN_DOWN_GROUP)]

    for b in range(groups[0][1]):
        up_proj(b)
    for gi, (fb, group) in enumerate(groups):
        if gi + 1 < len(groups):
            nxt, n_nxt = groups[gi + 1]
            for b in range(nxt, nxt + n_nxt):
                up_proj(b)
        acts = []
        for b in range(fb, fb + group):
            acts += act_block(b)
        contrib = _dot(jnp.concatenate(acts, axis=1), wdn_ref[fb * MXU_N:(fb + group) * MXU_N, :])
        if gi == 0:
            acc_s[...] = contrib
        else:
            acc_s[...] += contrib

    for lb in range(nlb):
        tail = u_s[lb, pl.ds(rows + first, FFN_HIST), :]
        u_s[lb, pl.ds(first, FFN_HIST), :] = tail
        bufo_ref[:, lb * LANES:(lb + 1) * LANES] = tail

    step = min(NORM_ROWS, rows)
    for g in range(rows // step):
        rs = pl.ds(g * step, step)
        y = x_ref[rs, :] + (acc_s[rs, :] + bdn_ref[...])
        if final_norm:
            y = _rmsnorm_rows(y, fw_ref[...])
        xo_ref[rs, :] = y


def _conv_ffn(x, buf, nw, wup, bup, wdw, bdw, wdn, bdn, fw, *, rows, final_norm):
    B, S, D = x.shape
    F = D_FF
    grid = (B, S // rows)
    x_spec = pl.BlockSpec((None, rows, D), lambda b, s: (b, s, 0))
    buf_spec = pl.BlockSpec((None, FFN_HIST, 2 * F), lambda b, s: (b, 0, 0))
    return pl.pallas_call(
        functools.partial(_ffn_kernel, rows=rows, final_norm=final_norm),
        grid=grid,
        in_specs=[x_spec, buf_spec, _const_spec((1, D)), _const_spec((D, 2 * F)),
                  _const_spec((1, 2 * F)), _const_spec((FFN_K, 2 * F)), _const_spec((1, 2 * F)),
                  _const_spec((F, D)), _const_spec((1, D)), _const_spec((1, D))],
        out_specs=[x_spec, buf_spec],
        out_shape=[jax.ShapeDtypeStruct(x.shape, x.dtype),
                   jax.ShapeDtypeStruct(buf.shape, buf.dtype)],
        scratch_shapes=[pltpu.VMEM((rows, D), BF16),
                        pltpu.VMEM((2 * F // LANES, FFN_PAD + rows, LANES), F32),
                        pltpu.VMEM((rows, D), F32)],
        compiler_params=_params(),
        name="conv_ffn",
    )(x, buf, _row2(nw), wup.astype(BF16), _row2(bup), wdw, _row2(bdw), wdn.astype(BF16),
      _row2(bdn), _row2(fw))


def _trunk(x, conv_bufs, hgrn_states, ffn_bufs, P, rows):
    depth = P['norm_mix_w'].shape[0]
    conv_new, hgrn_new, ffn_new = [], [], []
    for i in range(depth):
        j = i // 2
        if i % 2 == 0:
            x, nb = _conv_mixer(x, conv_bufs[j], P['norm_mix_w'][i], P['cv_w_pw1'][j],
                                P['cv_b_pw1'][j], P['cv_w_dw'][j], P['cv_b_dw'][j], P['cv_ln_g'][j],
                                P['cv_ln_b'][j], P['cv_w_pw2'][j], P['cv_b_pw2'][j], rows=rows)
            conv_new.append(nb)
        else:
            x, ns = _hgrn_mixer(x, hgrn_states[j], P['norm_mix_w'][i], P['hg_w_in'][j],
                                P['hg_lb_logits'], P['hg_norm_w'][j], P['hg_w_o'][j],
                                rows=rows, layer=j)
            hgrn_new.append(ns)
        x, fb = _conv_ffn(x, ffn_bufs[i], P['norm_ffn_w'][i], P['ff_w_up'][i], P['ff_b_up'][i],
                          P['ff_w_dw'][i], P['ff_b_dw'][i], P['ff_w_down'][i], P['ff_b_down'][i],
                          P['norm_final_w'], rows=rows, final_norm=(i == depth - 1))
        ffn_new.append(fb)
    return x, jnp.stack(conv_new), jnp.stack(hgrn_new), jnp.stack(ffn_new)


def kernel(x_prompt, x_sample, cache_conv, state_hgrn, cache_ffn, norm_mix_w, norm_ffn_w, norm_final_w, cv_w_pw1, cv_b_pw1, cv_w_dw, cv_b_dw, cv_ln_g, cv_ln_b, cv_w_pw2, cv_b_pw2, hg_w_in, hg_lb_logits, hg_norm_w, hg_w_o, ff_w_up, ff_b_up, ff_w_dw, ff_b_dw, ff_w_down, ff_b_down):
    P = dict(norm_mix_w=norm_mix_w, norm_ffn_w=norm_ffn_w, norm_final_w=norm_final_w,
             cv_w_pw1=cv_w_pw1, cv_b_pw1=cv_b_pw1, cv_w_dw=cv_w_dw, cv_b_dw=cv_b_dw,
             cv_ln_g=cv_ln_g, cv_ln_b=cv_ln_b, cv_w_pw2=cv_w_pw2, cv_b_pw2=cv_b_pw2,
             hg_w_in=hg_w_in, hg_lb_logits=hg_lb_logits, hg_norm_w=hg_norm_w, hg_w_o=hg_w_o,
             ff_w_up=ff_w_up, ff_b_up=ff_b_up, ff_w_dw=ff_w_dw, ff_b_dw=ff_b_dw,
             ff_w_down=ff_w_down, ff_b_down=ff_b_down)
    bp, sp, _ = x_prompt.shape
    n_conv = cache_conv.shape[0]
    n_hgrn = state_hgrn.shape[0]
    depth = cache_ffn.shape[0]
    zero_conv = jnp.zeros((n_conv, bp, CONV_HIST, D_MODEL), x_prompt.dtype)
    zero_hgrn = jnp.zeros((n_hgrn, bp, HG_HEADS, HG_DH, HG_DH), F32)
    zero_ffn = jnp.zeros((depth, bp, FFN_HIST, 2 * D_FF), x_prompt.dtype)
    y_p, conv_p, hgrn_p, ffn_p = _trunk(x_prompt, zero_conv, zero_hgrn, zero_ffn, P,
                                        min(PROMPT_TILE, sp))
    y_s, conv_s, hgrn_s, ffn_s = _trunk(x_sample, cache_conv, state_hgrn, cache_ffn, P,
                                        x_sample.shape[1])
    return (y_p, y_s, conv_p, hgrn_p, ffn_p, conv_s, hgrn_s, ffn_s)
```

```python
import functools

import jax
import jax.numpy as jnp
from jax import lax
from jax.experimental import pallas as pl
from jax.experimental.pallas import tpu as pltpu

D_MODEL = 1024
CONV_K = 31
HG_HEADS = 8
HG_DH = D_MODEL // HG_HEADS
D_FF = 2816
FFN_K = 3
EPS = 1e-6

LANES = 128
SUBLANES = 8
MXU_N = 256
VMEM_LIMIT_BYTES = 56 * 1024 * 1024

CONV_HIST = CONV_K - 1
CONV_PAD = 32
FFN_HIST = FFN_K - 1
FFN_PAD = SUBLANES
PROMPT_TILE = 512
HG_CHUNK = 64
MIN_HALF_CHUNK_LOG_DECAY = -60.0
ROW_GROUP = 64
NORM_ROWS = 32
LN_ROWS = 16
FFN_DOWN_GROUP = 6

F32 = jnp.float32
BF16 = jnp.bfloat16


def _dot(a, b):
    return jnp.dot(a, b, preferred_element_type=F32)


def _dot_nt(a, b):
    return lax.dot_general(a, b, (((1,), (1,)), ((), ())), preferred_element_type=F32)


def _dot_tn(a, b):
    return lax.dot_general(a, b, (((0,), (0,)), ((), ())), preferred_element_type=F32)


def _rmsnorm_rows(x, w):
    ms = jnp.mean(x * x, axis=-1, keepdims=True)
    return (x * lax.rsqrt(ms + EPS)) * w


def _rmsnorm_to(x_ref, w_ref, h_ref, rows):
    step = min(NORM_ROWS, rows)
    for g in range(rows // step):
        rs = pl.ds(g * step, step)
        h_ref[rs, :] = _rmsnorm_rows(x_ref[rs, :], w_ref[...]).astype(BF16)


def _conv_mixer_kernel(x_ref, hist_ref, nw_ref, w1_ref, b1_ref, wdw_ref, bdw_ref, lng_ref, lnb_ref,
                       w2_ref, b2_ref, xo_ref, histo_ref, h_s, p_s, y_s, z_s, *, rows):
    D = D_MODEL
    nlb = D // LANES
    first = CONV_PAD - CONV_HIST

    @pl.when(pl.program_id(1) == 0)
    def _():
        for lb in range(nlb):
            p_s[lb, pl.ds(first, CONV_HIST), :] = hist_ref[:, lb * LANES:(lb + 1) * LANES]

    _rmsnorm_to(x_ref, nw_ref, h_s, rows)

    h = h_s[...]
    per = MXU_N // LANES
    rg = min(ROW_GROUP, rows)
    ncb = D // MXU_N

    def pointwise_glu(cb):
        ca = slice(cb * MXU_N, (cb + 1) * MXU_N)
        cg = slice(D + cb * MXU_N, D + (cb + 1) * MXU_N)
        ua = _dot(h, w1_ref[:, ca]) + b1_ref[:, ca]
        ug = _dot(h, w1_ref[:, cg]) + b1_ref[:, cg]
        glu = ua * jax.nn.sigmoid(ug)
        for j in range(per):
            p_s[cb * per + j, pl.ds(CONV_PAD, rows), :] = glu[:, j * LANES:(j + 1) * LANES]

    def depthwise(lb):
        taps = [jnp.broadcast_to(wdw_ref[lb, k:k + 1, :], (rg, LANES)) for k in range(CONV_K)]
        bias = jnp.broadcast_to(bdw_ref[lb], (rg, LANES))
        for g in range(rows // rg):
            acc = bias
            for k in range(CONV_K):
                acc = acc + taps[k] * p_s[lb, pl.ds(g * rg + first + k, rg), :]
            y_s[lb, pl.ds(g * rg, rg), :] = acc

    pointwise_glu(0)
    for cb in range(ncb):
        if cb + 1 < ncb:
            pointwise_glu(cb + 1)
        for j in range(per):
            depthwise(cb * per + j)

    nr = min(LN_ROWS, rows)
    for g in range(rows // nr):
        rs = pl.ds(g * nr, nr)
        y = jnp.concatenate([y_s[lb, rs, :] for lb in range(nlb)], axis=1)
        mu = jnp.mean(y, axis=-1, keepdims=True)
        d = y - mu
        var = jnp.mean(d * d, axis=-1, keepdims=True)
        yn = (d * lax.rsqrt(var + EPS)) * lng_ref[...] + lnb_ref[...]
        z_s[rs, :] = (yn * jax.nn.sigmoid(yn)).astype(BF16)

    for lb in range(nlb):
        tail = p_s[lb, pl.ds(rows + first, CONV_HIST), :]
        p_s[lb, pl.ds(first, CONV_HIST), :] = tail
        histo_ref[:, lb * LANES:(lb + 1) * LANES] = tail

    out = _dot(z_s[...], w2_ref[...]) + b2_ref[...]
    xo_ref[...] = x_ref[...] + out


def _const_spec(shape):
    zeros = (0,) * len(shape)
    return pl.BlockSpec(shape, lambda b, s: zeros, pipeline_mode=pl.Buffered(1))


def _params():
    return pltpu.CompilerParams(dimension_semantics=("arbitrary", "arbitrary"),
                                vmem_limit_bytes=VMEM_LIMIT_BYTES)


def _row2(v):
    return v.reshape(1, -1)


def _lane_blocks(v):
    r, d = v.shape
    return v.reshape(r, d // LANES, LANES).transpose(1, 0, 2)


def _conv_mixer(x, hist, nw, w1, b1, wdw, bdw, lng, lnb, w2, b2, *, rows):
    B, S, D = x.shape
    grid = (B, S // rows)
    x_spec = pl.BlockSpec((None, rows, D), lambda b, s: (b, s, 0))
    hist_spec = pl.BlockSpec((None, CONV_HIST, D), lambda b, s: (b, 0, 0))
    return pl.pallas_call(
        functools.partial(_conv_mixer_kernel, rows=rows),
        grid=grid,
        in_specs=[x_spec, hist_spec, _const_spec((1, D)), _const_spec((D, 2 * D)),
                  _const_spec((1, 2 * D)), _const_spec((D // LANES, CONV_K, LANES)),
                  _const_spec((D // LANES, 1, LANES)), _const_spec((1, D)), _const_spec((1, D)),
                  _const_spec((D, D)), _const_spec((1, D))],
        out_specs=[x_spec, hist_spec],
        out_shape=[jax.ShapeDtypeStruct(x.shape, x.dtype),
                   jax.ShapeDtypeStruct(hist.shape, hist.dtype)],
        scratch_shapes=[pltpu.VMEM((rows, D), BF16),
                        pltpu.VMEM((D // LANES, CONV_PAD + rows, LANES), F32),
                        pltpu.VMEM((D // LANES, rows, LANES), F32),
                        pltpu.VMEM((rows, D), BF16)],
        compiler_params=_params(),
        name="conv_mixer",
    )(x, hist, _row2(nw), w1.astype(BF16), _row2(b1), _lane_blocks(wdw), _lane_blocks(_row2(bdw)),
      _row2(lng), _row2(lnb), w2.astype(BF16), _row2(b2))


def _hgrn_mixer_kernel(x_ref, s0_ref, nw_ref, win_ref, lbl_ref, hnw_ref, wo_ref, xo_ref, so_ref,
                       h_s, q_s, k_s, i_s, lf_s, bc_s, g_s, o_s, z_s, st_s, sb_s, *, rows, chunk, layer):
    D = D_MODEL
    nh = HG_HEADS
    dh = HG_DH

    @pl.when(pl.program_id(1) == 0)
    def _():
        for hd in range(nh):
            st_s[hd] = s0_ref[hd].T
            sb_s[hd] = s0_ref[hd].astype(BF16)

    _rmsnorm_to(x_ref, nw_ref, h_s, rows)

    logits = lbl_ref[...]
    m = jnp.max(logits, axis=0, keepdims=True)
    e = jnp.exp(logits - m)
    p = e / jnp.sum(e, axis=0, keepdims=True)
    csum = p[0:1, :]
    for j in range(1, layer + 1):
        csum = csum + p[j:j + 1, :]
    lb = csum - p[0:1, :]

    h = h_s[...]
    for cb in range(D // MXU_N):
        cols = slice(cb * MXU_N, (cb + 1) * MXU_N)
        q_s[:, cols] = _dot(h, win_ref[:, cb * MXU_N:(cb + 1) * MXU_N])
        fl = _dot(h, win_ref[:, D + cb * MXU_N:D + (cb + 1) * MXU_N])
        f = lb[:, cols] + (1.0 - lb[:, cols]) * jax.nn.sigmoid(fl)
        lf_s[:, cols] = jnp.log(f)
        k_s[:, cols] = 1.0 - f
        i_s[:, cols] = _dot(h, win_ref[:, 2 * D + cb * MXU_N:2 * D + (cb + 1) * MXU_N]).astype(BF16)
        gt = _dot(h, win_ref[:, 3 * D + cb * MXU_N:3 * D + (cb + 1) * MXU_N])
        g_s[:, cols] = hnw_ref[:, cols] * (gt * jax.nn.sigmoid(gt))

    ri = lax.broadcasted_iota(jnp.int32, (chunk, chunk), 0)
    ci = lax.broadcasted_iota(jnp.int32, (chunk, chunk), 1)
    causal = ri >= ci
    tri = causal.astype(BF16)

    mid = chunk // 2
    mn = jnp.zeros((1, D), F32)
    for c in range(rows // chunk):
        rs = pl.ds(c * chunk, chunk)
        lf = lf_s[rs, :]
        lf_hi = lf.astype(BF16)
        lf_lo = (lf - lf_hi.astype(F32)).astype(BF16)
        bc = _dot(tri, lf_hi) + _dot(tri, lf_lo)
        bc_s[rs, :] = bc
        b_mid = bc[mid - 1:mid, :]
        mn = jnp.minimum(mn, jnp.minimum(b_mid, bc[chunk - 1:chunk, :] - b_mid))
    safe = jnp.min(mn) >= MIN_HALF_CHUNK_LOG_DECAY

    def chunk_step(c):
        rs = pl.ds(c * chunk, chunk)
        for hd in range(nh):
            cols = slice(hd * dh, (hd + 1) * dh)
            bc = bc_s[rs, cols]
            b_mid = bc[mid - 1:mid, :]
            b_last = bc[chunk - 1:chunk, :]
            q = q_s[rs, cols]
            kk = k_s[rs, cols]
            qi = (q * jnp.exp(bc)).astype(BF16)
            qd = (q * jnp.exp(bc - b_mid)).astype(BF16)
            kd = (kk * jnp.exp(b_mid - bc)).astype(BF16)
            ke = (kk * jnp.exp(b_last - bc)).astype(BF16)
            iv = i_s[rs, cols]
            sc = jnp.where(causal, _dot_nt(qd, kd), 0.0)
            o_s[rs, cols] = _dot(sc.astype(BF16), iv) + _dot(qi, sb_s[hd])
            st = st_s[hd] * jnp.exp(b_last) + _dot_tn(iv, ke)
            st_s[hd] = st
            sb_s[hd] = st.astype(BF16).T

    row_id = lax.broadcasted_iota(jnp.int32, (chunk, 1), 0)

    def slow_chunk_body(c, carry):
        r = pl.multiple_of(c * chunk, chunk)
        lf = lf_s[pl.ds(r, chunk), :]
        qc = q_s[pl.ds(r, chunk), :].astype(BF16)
        kc = k_s[pl.ds(r, chunk), :].astype(BF16)
        iv = i_s[pl.ds(r, chunk), :]
        o_s[pl.ds(r, chunk), :] = jnp.zeros((chunk, D), F32)

        def token_body(t, carry2):
            sel = row_id == t
            f_row = jnp.exp(jnp.sum(jnp.where(sel, lf, 0.0), axis=0, keepdims=True))
            i_sel = jnp.where(sel, iv, jnp.zeros_like(iv))
            for hd in range(nh):
                cols = slice(hd * dh, (hd + 1) * dh)
                st = st_s[hd] * f_row[:, cols] + _dot_tn(i_sel[:, cols], kc[:, cols])
                st_s[hd] = st
                res = _dot_nt(qc[:, cols], st.astype(BF16))
                o_s[pl.ds(r, chunk), cols] = jnp.where(sel, res, o_s[pl.ds(r, chunk), cols])
            return carry2

        lax.fori_loop(0, chunk, token_body, 0)
        return carry

    @pl.when(safe)
    def _():
        for c in range(rows // chunk):
            chunk_step(c)

    @pl.when(jnp.logical_not(safe))
    def _():
        lax.fori_loop(0, rows // chunk, slow_chunk_body, 0)
        for hd in range(nh):
            sb_s[hd] = st_s[hd].astype(BF16).T

    nr = min(NORM_ROWS, rows)
    for g in range(rows // nr):
        rs = pl.ds(g * nr, nr)
        for hd in range(nh):
            cols = slice(hd * dh, (hd + 1) * dh)
            o = o_s[rs, cols]
            ms = jnp.mean(o * o, axis=-1, keepdims=True)
            on = (o * lax.rsqrt(ms + EPS)) * g_s[rs, cols]
            z_s[rs, cols] = on.astype(BF16)

    xo_ref[...] = x_ref[...] + _dot(z_s[...], wo_ref[...])

    @pl.when(pl.program_id(1) == pl.num_programs(1) - 1)
    def _():
        for hd in range(nh):
            so_ref[hd] = st_s[hd].T


def _hgrn_mixer(x, s0, nw, win, lb_logits, hnw, wo, *, rows, layer):
    B, S, D = x.shape
    chunk = min(HG_CHUNK, rows)
    grid = (B, S // rows)
    x_spec = pl.BlockSpec((None, rows, D), lambda b, s: (b, s, 0))
    st_spec = pl.BlockSpec((None, HG_HEADS, HG_DH, HG_DH), lambda b, s: (b, 0, 0, 0))
    n_layers = lb_logits.shape[0]
    return pl.pallas_call(
        functools.partial(_hgrn_mixer_kernel, rows=rows, chunk=chunk, layer=layer),
        grid=grid,
        in_specs=[x_spec, st_spec, _const_spec((1, D)), _const_spec((D, 4 * D)),
                  _const_spec((n_layers, D)), _const_spec((1, D)), _const_spec((D, D))],
        out_specs=[x_spec, st_spec],
        out_shape=[jax.ShapeDtypeStruct(x.shape, x.dtype),
                   jax.ShapeDtypeStruct(s0.shape, s0.dtype)],
        scratch_shapes=[pltpu.VMEM((rows, D), BF16),
                        pltpu.VMEM((rows, D), F32),
                        pltpu.VMEM((rows, D), F32),
                        pltpu.VMEM((rows, D), BF16),
                        pltpu.VMEM((rows, D), F32),
                        pltpu.VMEM((rows, D), F32),
                        pltpu.VMEM((rows, D), F32),
                        pltpu.VMEM((rows, D), F32),
                        pltpu.VMEM((rows, D), BF16),
                        pltpu.VMEM((HG_HEADS, HG_DH, HG_DH), F32),
                        pltpu.VMEM((HG_HEADS, HG_DH, HG_DH), BF16)],
        compiler_params=_params(),
        name="hgrn_mixer",
    )(x, s0, _row2(nw), win.astype(BF16), lb_logits, _row2(hnw), wo.astype(BF16))


def _ffn_kernel(x_ref, buf_ref, nw_ref, wup_ref, bup_ref, wdw_ref, bdw_ref, wdn_ref, bdn_ref, fw_ref,
                xo_ref, bufo_ref, h_s, u_s, acc_s, *, rows, final_norm):
    F = D_FF
    first = FFN_PAD - FFN_HIST
    nlb = 2 * F // LANES
    per = MXU_N // LANES
    nfb = F // MXU_N

    @pl.when(pl.program_id(1) == 0)
    def _():
        for lb in range(nlb):
            u_s[lb, pl.ds(first, FFN_HIST), :] = buf_ref[:, lb * LANES:(lb + 1) * LANES]

    _rmsnorm_to(x_ref, nw_ref, h_s, rows)

    def conv_piece(lb):
        cols = slice(lb * LANES, (lb + 1) * LANES)
        c = bdw_ref[:, cols]
        for k in range(FFN_K):
            c = c + wdw_ref[k:k + 1, cols] * u_s[lb, pl.ds(first + k, rows), :]
        return c

    h = h_s[...]

    def up_proj(b):
        for part in range(2):
            cols = slice(part * F + b * MXU_N, part * F + (b + 1) * MXU_N)
            u = _dot(h, wup_ref[:, cols]) + bup_ref[:, cols]
            for j in range(per):
                lb = (part * F + b * MXU_N) // LANES + j
                u_s[lb, pl.ds(FFN_PAD, rows), :] = u[:, j * LANES:(j + 1) * LANES]

    def act_block(b):
        out = []
        for j in range(per):
            a = conv_piece(b * per + j)
            v = conv_piece(F // LANES + b * per + j)
            out.append(((a * jax.nn.sigmoid(a)) * v).astype(BF16))
        return out

    groups = [(fb, min(FFN_DOWN_GROUP, nfb - fb)) for fb in range(0, nfb, FFN_DOWN_GROUP)]

    for b in range(groups[0][1]):
        up_proj(b)
    for gi, (fb, group) in enumerate(groups):
        if gi + 1 < len(groups):
            nxt, n_nxt = groups[gi + 1]
            for b in range(nxt, nxt + n_nxt):
                up_proj(b)
        acts = []
        for b in range(fb, fb + group):
            acts += act_block(b)
        contrib = _dot(jnp.concatenate(acts, axis=1), wdn_ref[fb * MXU_N:(fb + group) * MXU_N, :])
        if gi == 0:
            acc_s[...] = contrib
        else:
            acc_s[...] += contrib

    for lb in range(nlb):
        tail = u_s[lb, pl.ds(rows + first, FFN_HIST), :]
        u_s[lb, pl.ds(first, FFN_HIST), :] = tail
        bufo_ref[:, lb * LANES:(lb + 1) * LANES] = tail

    step = min(NORM_ROWS, rows)
    for g in range(rows // step):
        rs = pl.ds(g * step, step)
        y = x_ref[rs, :] + (acc_s[rs, :] + bdn_ref[...])
        if final_norm:
            y = _rmsnorm_rows(y, fw_ref[...])
        xo_ref[rs, :] = y


def _conv_ffn(x, buf, nw, wup, bup, wdw, bdw, wdn, bdn, fw, *, rows, final_norm):
    B, S, D = x.shape
    F = D_FF
    grid = (B, S // rows)
    x_spec = pl.BlockSpec((None, rows, D), lambda b, s: (b, s, 0))
    buf_spec = pl.BlockSpec((None, FFN_HIST, 2 * F), lambda b, s: (b, 0, 0))
    return pl.pallas_call(
        functools.partial(_ffn_kernel, rows=rows, final_norm=final_norm),
        grid=grid,
        in_specs=[x_spec, buf_spec, _const_spec((1, D)), _const_spec((D, 2 * F)),
                  _const_spec((1, 2 * F)), _const_spec((FFN_K, 2 * F)), _const_spec((1, 2 * F)),
                  _const_spec((F, D)), _const_spec((1, D)), _const_spec((1, D))],
        out_specs=[x_spec, buf_spec],
        out_shape=[jax.ShapeDtypeStruct(x.shape, x.dtype),
                   jax.ShapeDtypeStruct(buf.shape, buf.dtype)],
        scratch_shapes=[pltpu.VMEM((rows, D), BF16),
                        pltpu.VMEM((2 * F // LANES, FFN_PAD + rows, LANES), F32),
                        pltpu.VMEM((rows, D), F32)],
        compiler_params=_params(),
        name="conv_ffn",
    )(x, buf, _row2(nw), wup.astype(BF16), _row2(bup), wdw, _row2(bdw), wdn.astype(BF16),
      _row2(bdn), _row2(fw))


def _trunk(x, conv_bufs, hgrn_states, ffn_bufs, P, rows):
    depth = P['norm_mix_w'].shape[0]
    conv_new, hgrn_new, ffn_new = [], [], []
    for i in range(depth):
        j = i // 2
        if i % 2 == 0:
            x, nb = _conv_mixer(x, conv_bufs[j], P['norm_mix_w'][i], P['cv_w_pw1'][j],
                                P['cv_b_pw1'][j], P['cv_w_dw'][j], P['cv_b_dw'][j], P['cv_ln_g'][j],
                                P['cv_ln_b'][j], P['cv_w_pw2'][j], P['cv_b_pw2'][j], rows=rows)
            conv_new.append(nb)
        else:
            x, ns = _hgrn_mixer(x, hgrn_states[j], P['norm_mix_w'][i], P['hg_w_in'][j],
                                P['hg_lb_logits'], P['hg_norm_w'][j], P['hg_w_o'][j],
                                rows=rows, layer=j)
            hgrn_new.append(ns)
        x, fb = _conv_ffn(x, ffn_bufs[i], P['norm_ffn_w'][i], P['ff_w_up'][i], P['ff_b_up'][i],
                          P['ff_w_dw'][i], P['ff_b_dw'][i], P['ff_w_down'][i], P['ff_b_down'][i],
                          P['norm_final_w'], rows=rows, final_norm=(i == depth - 1))
        ffn_new.append(fb)
    return x, jnp.stack(conv_new), jnp.stack(hgrn_new), jnp.stack(ffn_new)


def kernel(x_prompt, x_sample, cache_conv, state_hgrn, cache_ffn, norm_mix_w, norm_ffn_w, norm_final_w, cv_w_pw1, cv_b_pw1, cv_w_dw, cv_b_dw, cv_ln_g, cv_ln_b, cv_w_pw2, cv_b_pw2, hg_w_in, hg_lb_logits, hg_norm_w, hg_w_o, ff_w_up, ff_b_up, ff_w_dw, ff_b_dw, ff_w_down, ff_b_down):
    P = dict(norm_mix_w=norm_mix_w, norm_ffn_w=norm_ffn_w, norm_final_w=norm_final_w,
             cv_w_pw1=cv_w_pw1, cv_b_pw1=cv_b_pw1, cv_w_dw=cv_w_dw, cv_b_dw=cv_b_dw,
             cv_ln_g=cv_ln_g, cv_ln_b=cv_ln_b, cv_w_pw2=cv_w_pw2, cv_b_pw2=cv_b_pw2,
             hg_w_in=hg_w_in, hg_lb_logits=hg_lb_logits, hg_norm_w=hg_norm_w, hg_w_o=hg_w_o,
             ff_w_up=ff_w_up, ff_b_up=ff_b_up, ff_w_dw=ff_w_dw, ff_b_dw=ff_b_dw,
             ff_w_down=ff_w_down, ff_b_down=ff_b_down)
    bp, sp, _ = x_prompt.shape
    n_conv = cache_conv.shape[0]
    n_hgrn = state_hgrn.shape[0]
    depth = cache_ffn.shape[0]
    zero_conv = jnp.zeros((n_conv, bp, CONV_HIST, D_MODEL), x_prompt.dtype)
    zero_hgrn = jnp.zeros((n_hgrn, bp, HG_HEADS, HG_DH, HG_DH), F32)
    zero_ffn = jnp.zeros((depth, bp, FFN_HIST, 2 * D_FF), x_prompt.dtype)
    y_p, conv_p, hgrn_p, ffn_p = _trunk(x_prompt, zero_conv, zero_hgrn, zero_ffn, P,
                                        min(PROMPT_TILE, sp))
    y_s, conv_s, hgrn_s, ffn_s = _trunk(x_sample, cache_conv, state_hgrn, cache_ffn, P,
                                        x_sample.shape[1])
    return (y_p, y_s, conv_p, hgrn_p, ffn_p, conv_s, hgrn_s, ffn_s)
```
